```python
import jax, jax.numpy as jnp
from jax import lax
import numpy as np

D_MODEL = 4096
BATCH = 4
SEQ = 2048
DEPTH = 1
DEC_BATCH = 16
DEC_SEQ = 64
PAST_LEN = 2048

CHUNK = 64
ATTN_HEADS = 32
ATTN_KV_HEADS = 4
ATTN_GROUP = ATTN_HEADS // ATTN_KV_HEADS
ATTN_HEAD_DIM = 64
WINDOW = 128
WIN_CHUNKS = WINDOW // CHUNK
ATTN_Q_W = ATTN_HEADS * ATTN_HEAD_DIM
ATTN_KV_W = ATTN_KV_HEADS * ATTN_HEAD_DIM
RET_HEADS = 8
RET_HEAD_DIM = 256
RET_W = RET_HEADS * RET_HEAD_DIM
IN_WIDTHS = (ATTN_Q_W, ATTN_KV_W, ATTN_KV_W, RET_W, RET_W, RET_W, RET_W, D_MODEL, D_MODEL)
IN_TOTAL = sum(IN_WIDTHS)
N_EXPERTS = 32
TOP_K = 4
D_FF = 4096
SWIGLU_ALPHA = 1.702
SWIGLU_LIMIT = 7.0
MOE_BLOCK = 128
RMS_EPS = 1e-5
NEG_INF = -1e30

kernel_name = "hybrid_swa_retention_moe_stream_step"


def rms_norm(x, g):
    xf = x.astype(jnp.float32)
    y = xf * lax.rsqrt(jnp.mean(xf * xf, axis=-1, keepdims=True) + RMS_EPS)
    return (y * g.astype(jnp.float32)).astype(x.dtype)


def alibi_slopes():
    return 2.0 ** (-8.0 * jnp.arange(1, ATTN_HEADS + 1, dtype=jnp.float32) / ATTN_HEADS)


def ret_log_decay():
    return jnp.log1p(-(2.0 ** (-5.0 - jnp.arange(RET_HEADS, dtype=jnp.float32))))


def mixer_project(h, w_in):
    B, L, _ = h.shape
    z = h @ w_in
    parts = []
    off = 0
    for w in IN_WIDTHS:
        parts.append(z[..., off:off + w])
        off += w
    aq, ak, av, rq, rk, rv, rg, ga, gr = parts
    aq = aq.reshape(B, L, ATTN_KV_HEADS, ATTN_GROUP, ATTN_HEAD_DIM)
    ak = ak.reshape(B, L, ATTN_KV_HEADS, ATTN_HEAD_DIM)
    av = av.reshape(B, L, ATTN_KV_HEADS, ATTN_HEAD_DIM)
    rq = rq.reshape(B, L, RET_HEADS, RET_HEAD_DIM).astype(jnp.float32)
    rk = rk.reshape(B, L, RET_HEADS, RET_HEAD_DIM).astype(jnp.float32) * RET_HEAD_DIM ** -0.5
    rv = rv.reshape(B, L, RET_HEADS, RET_HEAD_DIM).astype(jnp.float32)
    return aq, ak, av, rq, rk, rv, rg, ga, gr


def banded_sink_attention(q, k, v, dist, valid, sinks):
    s = jnp.einsum('bnqhgd,bnkhd->bnhgqk', q, k).astype(jnp.float32) * ATTN_HEAD_DIM ** -0.5
    slopes = alibi_slopes().reshape(ATTN_KV_HEADS, ATTN_GROUP)
    s = s - slopes[None, None, :, :, None, None] * dist[None, :, None, None]
    s = jnp.where(valid[None, :, None, None], s, NEG_INF)
    sink = sinks.astype(jnp.float32).reshape(ATTN_KV_HEADS, ATTN_GROUP)[None, None, :, :, None, None]
    sink = jnp.broadcast_to(sink, s.shape[:-1] + (1,))
    p = jax.nn.softmax(jnp.concatenate([s, sink], axis=-1), axis=-1)[..., :-1]
    return jnp.einsum('bnhgqk,bnkhd->bnqhgd', p.astype(v.dtype), v)


def attn_prompt(q, k, v, sinks):
    B, S = q.shape[:2]
    n_chunks = S // CHUNK
    qc = q.reshape(B, n_chunks, CHUNK, ATTN_KV_HEADS, ATTN_GROUP, ATTN_HEAD_DIM)

    def bands(a):
        a = jnp.pad(a, ((0, 0), (WINDOW, 0), (0, 0), (0, 0)))
        a = a.reshape(B, n_chunks + WIN_CHUNKS, CHUNK, ATTN_KV_HEADS, ATTN_HEAD_DIM)
        return jnp.concatenate([a[:, i:i + n_chunks] for i in range(WIN_CHUNKS + 1)], axis=2)

    q_pos = jnp.arange(n_chunks)[:, None] * CHUNK + jnp.arange(CHUNK)[None, :]
    k_pos = jnp.arange(n_chunks)[:, None] * CHUNK - WINDOW + jnp.arange(WINDOW + CHUNK)[None, :]
    dist = jnp.abs(q_pos[:, :, None] - k_pos[:, None, :]).astype(jnp.float32)
    valid = jnp.broadcast_to(k_pos[:, None, :] >= 0, dist.shape)
    o = banded_sink_attention(qc, bands(k), bands(v), dist, valid, sinks)
    return o.reshape(B, S, ATTN_Q_W)


def attn_sample(q, k_new, v_new, k_cache, v_cache, sinks):
    B, L = q.shape[:2]
    n_past = k_cache.shape[1]
    k = jnp.concatenate([k_cache.astype(k_new.dtype), k_new], axis=1)[:, None]
    v = jnp.concatenate([v_cache.astype(v_new.dtype), v_new], axis=1)[:, None]
    q_pos = n_past + jnp.arange(L)
    k_pos = jnp.arange(n_past + L)
    dist = jnp.abs(q_pos[:, None] - k_pos[None, :]).astype(jnp.float32)[None]
    valid = jnp.ones(dist.shape, dtype=bool)
    o = banded_sink_attention(q[:, None], k, v, dist, valid, sinks)
    return o.reshape(B, L, ATTN_Q_W)


def retention_chunk(state, q, k, v, log_g):
    L = q.shape[1]
    idx = jnp.arange(L, dtype=jnp.float32)
    rel = idx[:, None] - idx[None, :]
    decay = jnp.where(rel >= 0, jnp.exp(log_g[:, None, None] * jnp.maximum(rel, 0.0)), 0.0)
    scores = jnp.einsum('blhd,bmhd->bhlm', q, k) * decay[None]
    o = jnp.einsum('bhlm,bmhe->blhe', scores, v)
    o = o + jnp.einsum('blhd,bhde->blhe', q, state) * jnp.exp((idx[:, None] + 1.0) * log_g[None, :])[None, :, :, None]
    k_dec = k * jnp.exp((L - 1.0 - idx)[:, None] * log_g[None, :])[None, :, :, None]
    state = state * jnp.exp(L * log_g)[None, :, None, None] + jnp.einsum('blhd,blhe->bhde', k_dec, v)
    return state, o


def retention_prompt(q, k, v, log_g):
    B, S = q.shape[:2]
    n_chunks = S // CHUNK

    def to_chunks(a):
        return a.reshape(B, n_chunks, CHUNK, RET_HEADS, RET_HEAD_DIM).swapaxes(0, 1)

    state0 = jnp.zeros((B, RET_HEADS, RET_HEAD_DIM, RET_HEAD_DIM), jnp.float32)

    def step(s, qkv):
        return retention_chunk(s, qkv[0], qkv[1], qkv[2], log_g)

    state, o = lax.scan(step, state0, (to_chunks(q), to_chunks(k), to_chunks(v)))
    return state, o.swapaxes(0, 1).reshape(B, S, RET_HEADS, RET_HEAD_DIM)


def mixer_merge(attn_o, ret_o, ret_gate, gate_attn, gate_ret, w_o_attn, w_o_ret, w_out):
    B, L = attn_o.shape[:2]
    r = ret_o * lax.rsqrt(jnp.mean(ret_o * ret_o, axis=-1, keepdims=True) + RMS_EPS)
    r = (r.reshape(B, L, RET_W) * jax.nn.silu(ret_gate.astype(jnp.float32))).astype(attn_o.dtype)
    y_attn = attn_o @ w_o_attn
    y_ret = r @ w_o_ret
    merged = jax.nn.sigmoid(gate_attn) * y_attn + jax.nn.sigmoid(gate_ret) * y_ret
    return merged @ w_out


def moe_ffn(h, w_router, b_router, w_gate_up, b_gate_up, w_down, b_down):
    B, L, D = h.shape
    n_tok = B * L
    t = h.reshape(n_tok, D)
    logits = (t @ w_router).astype(jnp.float32) + b_router.astype(jnp.float32)
    top_logit, top_e = lax.top_k(logits, TOP_K)
    top_w = jax.nn.softmax(top_logit, axis=-1)
    n_assign = n_tok * TOP_K
    flat_e = top_e.reshape(n_assign)
    flat_tok = jnp.repeat(jnp.arange(n_tok, dtype=jnp.int32), TOP_K)
    flat_w = top_w.reshape(n_assign)
    order = jnp.argsort(flat_e)
    se, stok, sw = flat_e[order], flat_tok[order], flat_w[order]
    counts = jnp.bincount(flat_e, length=N_EXPERTS)
    starts = jnp.cumsum(counts) - counts
    padded = (counts + MOE_BLOCK - 1) // MOE_BLOCK * MOE_BLOCK
    pstarts = jnp.cumsum(padded) - padded
    dest = pstarts[se] + jnp.arange(n_assign, dtype=jnp.int32) - starts[se]
    n_blocks = -(-n_assign // MOE_BLOCK) + N_EXPERTS
    n_rows = n_blocks * MOE_BLOCK
    row_tok = jnp.zeros((n_rows,), jnp.int32).at[dest].set(stok)
    row_w = jnp.zeros((n_rows,), jnp.float32).at[dest].set(sw)
    block_end = jnp.cumsum(padded) // MOE_BLOCK
    block_e = jnp.minimum(jnp.searchsorted(block_end, jnp.arange(n_blocks), side='right'), N_EXPERTS - 1)

    def expert_block(args):
        tok, e = args
        gu = t[tok] @ w_gate_up[e] + b_gate_up[e]
        glu = jnp.minimum(gu[:, :D_FF], SWIGLU_LIMIT)
        lin = jnp.clip(gu[:, D_FF:], -SWIGLU_LIMIT, SWIGLU_LIMIT)
        act = glu * jax.nn.sigmoid(SWIGLU_ALPHA * glu) * (lin + 1.0)
        return act @ w_down[e] + b_down[e]

    y_rows = lax.map(expert_block, (row_tok.reshape(n_blocks, MOE_BLOCK), block_e.astype(jnp.int32)))
    y_rows = y_rows.reshape(n_rows, D).astype(jnp.float32) * row_w[:, None]
    y = jax.ops.segment_sum(y_rows, row_tok, num_segments=n_tok)
    return y.astype(h.dtype).reshape(B, L, D)


def trunk_layer(x, past, log_g, norm_mix, w_in, attn_sinks, w_o_attn, w_o_ret, w_out,
                norm_ffn, w_router, b_router, w_gate_up, b_gate_up, w_down, b_down):
    h = rms_norm(x, norm_mix)
    aq, ak, av, rq, rk, rv, rg, ga, gr = mixer_project(h, w_in)
    if past is None:
        attn_o = attn_prompt(aq, ak, av, attn_sinks)
        ret_state, ret_o = retention_prompt(rq, rk, rv, log_g)
        n_keep = min(WINDOW, x.shape[1])
        new_k, new_v = ak[:, -n_keep:], av[:, -n_keep:]
    else:
        k_cache, v_cache, s_cache = past
        attn_o = attn_sample(aq, ak, av, k_cache, v_cache, attn_sinks)
        ret_state, ret_o = retention_chunk(s_cache.astype(jnp.float32), rq, rk, rv, log_g)
        new_k, new_v = ak, av
    x = x + mixer_merge(attn_o, ret_o, rg, ga, gr, w_o_attn, w_o_ret, w_out)
    x = x + moe_ffn(rms_norm(x, norm_ffn), w_router, b_router, w_gate_up, b_gate_up, w_down, b_down)
    return x, new_k, new_v, ret_state.astype(x.dtype)


def setup_inputs(seed: int = 0) -> dict:
    key = jax.random.key(seed)
    ks = jax.random.split(key, 24)
    f32 = jnp.float32

    def nrm(k, shape, scale):
        return jax.random.normal(k, shape, f32) * scale

    win_rows = min(WINDOW, PAST_LEN)
    return {
        "x_prompt": nrm(ks[0], (BATCH, SEQ, D_MODEL), 1.0),
        "x_sample": nrm(ks[1], (DEC_BATCH, DEC_SEQ, D_MODEL), 1.0),
        "cache_attn_k": nrm(ks[2], (DEPTH, DEC_BATCH, win_rows, ATTN_KV_HEADS, ATTN_HEAD_DIM), 1.0),
        "cache_attn_v": nrm(ks[3], (DEPTH, DEC_BATCH, win_rows, ATTN_KV_HEADS, ATTN_HEAD_DIM), 1.0),
        "state_ret": nrm(ks[4], (DEPTH, DEC_BATCH, RET_HEADS, RET_HEAD_DIM, RET_HEAD_DIM), 1.0),
        "norm_mix": 1.0 + nrm(ks[5], (DEPTH, D_MODEL), 0.02),
        "w_in": nrm(ks[6], (DEPTH, D_MODEL, IN_TOTAL), D_MODEL ** -0.5),
        "attn_sinks": nrm(ks[7], (DEPTH, ATTN_HEADS), 0.5),
        "w_o_attn": nrm(ks[8], (DEPTH, ATTN_Q_W, D_MODEL), ATTN_Q_W ** -0.5),
        "w_o_ret": nrm(ks[9], (DEPTH, RET_W, D_MODEL), RET_W ** -0.5),
        "w_out": nrm(ks[10], (DEPTH, D_MODEL, D_MODEL), D_MODEL ** -0.5),
        "norm_ffn": 1.0 + nrm(ks[11], (DEPTH, D_MODEL), 0.02),
        "w_router": nrm(ks[12], (DEPTH, D_MODEL, N_EXPERTS), D_MODEL ** -0.5),
        "b_router": nrm(ks[13], (DEPTH, N_EXPERTS), 0.01),
        "w_gate_up": nrm(ks[14], (DEPTH, N_EXPERTS, D_MODEL, 2 * D_FF), D_MODEL ** -0.5),
        "b_gate_up": nrm(ks[15], (DEPTH, N_EXPERTS, 2 * D_FF), 0.01),
        "w_down": nrm(ks[16], (DEPTH, N_EXPERTS, D_FF, D_MODEL), D_FF ** -0.5),
        "b_down": nrm(ks[17], (DEPTH, N_EXPERTS, D_MODEL), 0.01),
        "norm_final": 1.0 + nrm(ks[18], (D_MODEL,), 0.02),
    }


def reference(x_prompt, x_sample, cache_attn_k, cache_attn_v, state_ret, norm_mix, w_in, attn_sinks,
              w_o_attn, w_o_ret, w_out, norm_ffn, w_router, b_router, w_gate_up, b_gate_up,
              w_down, b_down, norm_final):
    log_g = ret_log_decay()
    xp, xs = x_prompt, x_sample
    kp, vp, sp, kn, vn, sn = [], [], [], [], [], []
    for l in range(DEPTH):
        params = (norm_mix[l], w_in[l], attn_sinks[l], w_o_attn[l], w_o_ret[l], w_out[l], norm_ffn[l],
                  w_router[l], b_router[l], w_gate_up[l], b_gate_up[l], w_down[l], b_down[l])
        xp, k_p, v_p, s_p = trunk_layer(xp, None, log_g, *params)
        xs, k_s, v_s, s_s = trunk_layer(xs, (cache_attn_k[l], cache_attn_v[l], state_ret[l]), log_g, *params)
        kp.append(k_p)
        vp.append(v_p)
        sp.append(s_p)
        kn.append(k_s)
        vn.append(v_s)
        sn.append(s_s)
    y_prompt = rms_norm(xp, norm_final)
    y_sample = rms_norm(xs, norm_final)
    return (y_prompt, y_sample, jnp.stack(kp), jnp.stack(vp), jnp.stack(sp), jnp.stack(kn), jnp.stack(vn), jnp.stack(sn))
```

```python
import functools

import jax
import jax.numpy as jnp
from jax import lax
from jax.experimental import pallas as pl
from jax.experimental.pallas import tpu as pltpu

F32 = jnp.float32
BF16 = jnp.bfloat16

CHUNK = 64
WINDOW = 128
ATTN_HEADS = 32
ATTN_KV_HEADS = 4
ATTN_GROUP = ATTN_HEADS // ATTN_KV_HEADS
ATTN_HEAD_DIM = 64
ATTN_Q_W = ATTN_HEADS * ATTN_HEAD_DIM
ATTN_KV_W = ATTN_KV_HEADS * ATTN_HEAD_DIM
RET_HEADS = 8
RET_HEAD_DIM = 256
RET_W = RET_HEADS * RET_HEAD_DIM
N_EXPERTS = 32
TOP_K = 4
SWIGLU_ALPHA = 1.702
SWIGLU_LIMIT = 7.0
RMS_EPS = 1e-5
NEG_INF = -1e30

ATTN_COLS = 256
RET_HEADS_PER_STEP = 2

VMEM_LIMIT_BYTES = 56 * 1024 * 1024

MOE_TM = 256
MOE_TN = 512
PROJ_TM = 1024
PROJ_TN = 512
COMBINE_TB = 128
ROUTER_TB = 256
NORM_TM = 256


def _params(*semantics):
    return pltpu.CompilerParams(dimension_semantics=semantics, vmem_limit_bytes=VMEM_LIMIT_BYTES)


def _rms(x, g):
    ms = jnp.mean(x * x, axis=-1, keepdims=True)
    return x * lax.rsqrt(ms + RMS_EPS) * g


def _sigmoid(x):
    return 1.0 / (1.0 + jnp.exp(-x))


def _rmsnorm_body(x_ref, g_ref, o_ref):
    o_ref[...] = _rms(x_ref[...], g_ref[...]).astype(o_ref.dtype)


def _rmsnorm(x, g, out_dtype):
    m, d = x.shape
    tm = NORM_TM
    return pl.pallas_call(
        _rmsnorm_body,
        grid=(m // tm,),
        in_specs=[pl.BlockSpec((tm, d), lambda i: (i, 0)),
                  pl.BlockSpec((1, d), lambda i: (0, 0))],
        out_specs=pl.BlockSpec((tm, d), lambda i: (i, 0)),
        out_shape=jax.ShapeDtypeStruct((m, d), out_dtype),
        compiler_params=_params("arbitrary"),
        name="rmsnorm",
    )(x, g.reshape(1, d))


def _matmul_body(x_ref, w_ref, *rest, has_res):
    if has_res:
        r_ref, o_ref, wb_ref = rest
    else:
        o_ref, wb_ref = rest

    @pl.when(pl.program_id(1) == 0)
    def _():
        wb_ref[...] = w_ref[...].astype(BF16)

    acc = jnp.dot(x_ref[...], wb_ref[...], preferred_element_type=F32)
    if has_res:
        acc = acc + r_ref[...]
    o_ref[...] = acc.astype(o_ref.dtype)


def _matmul(x, w, out_dtype, name, residual=None):
    m, k = x.shape
    n = w.shape[1]
    tm, tn = min(PROJ_TM, m), min(PROJ_TN, n)
    in_specs = [pl.BlockSpec((tm, k), lambda j, i: (i, 0)),
                pl.BlockSpec((k, tn), lambda j, i: (0, j))]
    args = [x, w]
    if residual is not None:
        in_specs.append(pl.BlockSpec((tm, tn), lambda j, i: (i, j)))
        args.append(residual)
    return pl.pallas_call(
        functools.partial(_matmul_body, has_res=residual is not None),
        grid=(n // tn, m // tm),
        in_specs=in_specs,
        out_specs=pl.BlockSpec((tm, tn), lambda j, i: (i, j)),
        out_shape=jax.ShapeDtypeStruct((m, n), out_dtype),
        scratch_shapes=[pltpu.VMEM((k, tn), BF16)],
        compiler_params=_params("arbitrary", "arbitrary"),
        name=name,
    )(*args)


def _merge_body(a_ref, r_ref, wa_ref, wr_ref, ga_ref, gr_ref, o_ref, wab_ref, wrb_ref):
    @pl.when(pl.program_id(1) == 0)
    def _():
        wab_ref[...] = wa_ref[...].astype(BF16)
        wrb_ref[...] = wr_ref[...].astype(BF16)

    ya = jnp.dot(a_ref[...], wab_ref[...], preferred_element_type=F32)
    yr = jnp.dot(r_ref[...], wrb_ref[...], preferred_element_type=F32)
    o_ref[...] = (_sigmoid(ga_ref[...]) * ya + _sigmoid(gr_ref[...]) * yr).astype(o_ref.dtype)


def _merge(a, r, w_a, w_r, z, ga_col, gr_col):
    m, ka = a.shape
    kr = r.shape[1]
    n = w_a.shape[1]
    tm, tn = min(PROJ_TM, m), min(PROJ_TN, n)
    ga_blk, gr_blk = ga_col // tn, gr_col // tn
    return pl.pallas_call(
        _merge_body,
        grid=(n // tn, m // tm),
        in_specs=[pl.BlockSpec((tm, ka), lambda j, i: (i, 0)),
                  pl.BlockSpec((tm, kr), lambda j, i: (i, 0)),
                  pl.BlockSpec((ka, tn), lambda j, i: (0, j)),
                  pl.BlockSpec((kr, tn), lambda j, i: (0, j)),
                  pl.BlockSpec((tm, tn), lambda j, i: (i, ga_blk + j)),
                  pl.BlockSpec((tm, tn), lambda j, i: (i, gr_blk + j))],
        out_specs=pl.BlockSpec((tm, tn), lambda j, i: (i, j)),
        out_shape=jax.ShapeDtypeStruct((m, n), BF16),
        scratch_shapes=[pltpu.VMEM((ka, tn), BF16), pltpu.VMEM((kr, tn), BF16)],
        compiler_params=_params("arbitrary", "arbitrary"),
        name="gated_merge",
    )(a, r, w_a, w_r, z, z)


def _attn_body(q_ref, k0_ref, k1_ref, k2_ref, v0_ref, v1_ref, v2_ref, bias_ref, o_ref, *, prompt):
    hd, grp = ATTN_HEAD_DIM, ATTN_GROUP
    rows = grp * CHUNK
    n_keys = WINDOW + CHUNK
    pad = jnp.zeros((ATTN_COLS - n_keys, hd), F32)
    if prompt:
        n_invalid = jnp.maximum(WINDOW - pl.program_id(1) * CHUNK, 0)
        col = lax.broadcasted_iota(jnp.int32, (rows, ATTN_COLS), 1)
    for h in range(ATTN_KV_HEADS):
        sl = slice(h * hd, (h + 1) * hd)
        kh = jnp.concatenate([k0_ref[:, sl], k1_ref[:, sl], k2_ref[:, sl], pad], axis=0).astype(BF16)
        vh = jnp.concatenate([v0_ref[:, sl], v1_ref[:, sl], v2_ref[:, sl], pad], axis=0).astype(BF16)
        qh = jnp.concatenate(
            [q_ref[:, (h * grp + g) * hd:(h * grp + g + 1) * hd] for g in range(grp)], axis=0).astype(BF16)
        s = lax.dot_general(qh, kh, (((1,), (1,)), ((), ())), preferred_element_type=F32)
        s = s * (hd ** -0.5) + bias_ref[h * rows:(h + 1) * rows, :]
        if prompt:
            s = jnp.where(col < n_invalid, NEG_INF, s)
        m = jnp.max(s, axis=-1, keepdims=True)
        p = jnp.exp(s - m)
        p = (p / jnp.sum(p, axis=-1, keepdims=True)).astype(BF16)
        oh = jnp.dot(p, vh, preferred_element_type=F32)
        for g in range(grp):
            off = (h * grp + g) * hd
            o_ref[:, off:off + hd] = oh[g * CHUNK:(g + 1) * CHUNK, :].astype(o_ref.dtype)


def _attn_bias(sinks):
    n_keys = WINDOW + CHUNK
    slopes = 2.0 ** (-8.0 * jnp.arange(1, ATTN_HEADS + 1, dtype=F32) / ATTN_HEADS)
    q_pos = WINDOW + jnp.arange(CHUNK)
    k_pos = jnp.arange(n_keys)
    dist = jnp.abs(q_pos[:, None] - k_pos[None, :]).astype(F32)
    alibi = -(slopes[:, None, None] * dist[None])
    sink = jnp.broadcast_to(sinks.astype(F32)[:, None, None], (ATTN_HEADS, CHUNK, 1))
    fill = jnp.full((ATTN_HEADS, CHUNK, ATTN_COLS - n_keys - 1), NEG_INF, F32)
    return jnp.concatenate([alibi, sink, fill], axis=-1).reshape(ATTN_HEADS * CHUNK, ATTN_COLS)


def _attention(z, bias, row_blk0, n_seq, n_chunks, k_col, v_col, cache_k=None, cache_v=None):
    kb, vb = k_col // ATTN_KV_W, v_col // ATTN_KV_W
    wc = WINDOW // CHUNK

    def q_map(b, c):
        return (row_blk0 + b * n_chunks + c, 0)

    if cache_k is None:
        def kv_map(j, colblk):
            return lambda b, c: (row_blk0 + b * n_chunks + jnp.maximum(c - wc + j, 0), colblk)
        kv_args = [z] * 6
        kv_specs = [pl.BlockSpec((CHUNK, ATTN_KV_W), kv_map(j, cb)) for cb in (kb, vb) for j in range(wc + 1)]
    else:
        def cache_map(j):
            return lambda b, c: (b * wc + j, 0)
        def new_map(colblk):
            return lambda b, c: (row_blk0 + b, colblk)
        kv_args = [cache_k, cache_k, z, cache_v, cache_v, z]
        kv_specs = []
        for cb in (kb, vb):
            kv_specs += [pl.BlockSpec((CHUNK, ATTN_KV_W), cache_map(j)) for j in range(wc)]
            kv_specs.append(pl.BlockSpec((CHUNK, ATTN_KV_W), new_map(cb)))
    return pl.pallas_call(
        functools.partial(_attn_body, prompt=cache_k is None),
        grid=(n_seq, n_chunks),
        in_specs=[pl.BlockSpec((CHUNK, ATTN_Q_W), q_map)] + kv_specs
                 + [pl.BlockSpec(bias.shape, lambda b, c: (0, 0))],
        out_specs=pl.BlockSpec((CHUNK, ATTN_Q_W), lambda b, c: (b * n_chunks + c, 0)),
        out_shape=jax.ShapeDtypeStruct((n_seq * n_chunks * CHUNK, ATTN_Q_W), BF16),
        compiler_params=_params("arbitrary", "arbitrary"),
        name="attn_prompt" if cache_k is None else "attn_sample",
    )(z, *kv_args, bias)


def _ret_body(*refs, has_init):
    if has_init:
        (q_ref, k_ref, v_ref, g_ref, decay_ref, dout_ref, kd_ref, g64_ref, s0_ref,
         r_ref, sout_ref, state_ref) = refs
    else:
        (q_ref, k_ref, v_ref, g_ref, decay_ref, dout_ref, kd_ref, g64_ref,
         r_ref, sout_ref, state_ref) = refs
    c = pl.program_id(2)
    hd = RET_HEAD_DIM

    @pl.when(c == 0)
    def _():
        if has_init:
            state_ref[...] = s0_ref[0]
        else:
            state_ref[...] = jnp.zeros(state_ref.shape, F32)

    for hh in range(RET_HEADS_PER_STEP):
        sl = slice(hh * hd, (hh + 1) * hd)
        q = q_ref[:, sl].astype(BF16)
        k = k_ref[:, sl] * (hd ** -0.5)
        v = v_ref[:, sl].astype(BF16)
        st = state_ref[hh]
        scores = lax.dot_general(q, k.astype(BF16), (((1,), (1,)), ((), ())),
                                 preferred_element_type=F32) * decay_ref[hh]
        o = jnp.dot(scores.astype(BF16), v, preferred_element_type=F32)
        o = o + jnp.dot(q, st.astype(BF16), preferred_element_type=F32) * dout_ref[hh]
        k_dec = (k * kd_ref[hh]).T.astype(BF16)
        state_ref[hh] = st * g64_ref[hh] + jnp.dot(k_dec, v, preferred_element_type=F32)
        rn = o * lax.rsqrt(jnp.mean(o * o, axis=-1, keepdims=True) + RMS_EPS)
        gate = g_ref[:, sl]
        r_ref[:, sl] = (rn * (gate * _sigmoid(gate))).astype(r_ref.dtype)

    @pl.when(c == pl.num_programs(2) - 1)
    def _():
        sout_ref[0] = state_ref[...]


def _ret_consts():
    log_g = jnp.log1p(-(2.0 ** (-5.0 - jnp.arange(RET_HEADS, dtype=F32))))
    idx = jnp.arange(CHUNK, dtype=F32)
    rel = idx[:, None] - idx[None, :]
    decay = jnp.where(rel >= 0, jnp.exp(log_g[:, None, None] * jnp.maximum(rel, 0.0)), 0.0)
    dout = jnp.exp((idx[None, :] + 1.0) * log_g[:, None])
    kd = jnp.exp((CHUNK - 1.0 - idx)[None, :] * log_g[:, None])
    g64 = jnp.exp(CHUNK * log_g)
    bc = lambda a: jnp.broadcast_to(a[:, :, None], (RET_HEADS, a.shape[1], RET_HEAD_DIM))
    return decay, bc(dout), bc(kd), bc(g64[:, None])


def _retention(z, consts, row_blk0, n_seq, n_chunks, q_col, k_col, v_col, g_col, state0=None):
    hps = RET_HEADS_PER_STEP
    w = hps * RET_HEAD_DIM
    n_hp = RET_HEADS // hps
    decay, dout, kd, g64 = consts

    def zmap(col):
        cb = col // w
        return lambda b, h, c: (row_blk0 + b * n_chunks + c, cb + h)

    cmap = lambda b, h, c: (h, 0, 0)
    in_specs = [pl.BlockSpec((CHUNK, w), zmap(col)) for col in (q_col, k_col, v_col, g_col)]
    in_specs += [pl.BlockSpec((hps, CHUNK, CHUNK), cmap),
                 pl.BlockSpec((hps, CHUNK, RET_HEAD_DIM), cmap),
                 pl.BlockSpec((hps, CHUNK, RET_HEAD_DIM), cmap),
                 pl.BlockSpec((hps, 1, RET_HEAD_DIM), cmap)]
    args = [z, z, z, z, decay, dout, kd, g64]
    st_spec = pl.BlockSpec((1, hps, RET_HEAD_DIM, RET_HEAD_DIM), lambda b, h, c: (b, h, 0, 0))
    if state0 is not None:
        in_specs.append(st_spec)
        args.append(state0)
    return pl.pallas_call(
        functools.partial(_ret_body, has_init=state0 is not None),
        grid=(n_seq, n_hp, n_chunks),
        in_specs=in_specs,
        out_specs=[pl.BlockSpec((CHUNK, w), lambda b, h, c: (b * n_chunks + c, h)), st_spec],
        out_shape=[jax.ShapeDtypeStruct((n_seq * n_chunks * CHUNK, RET_W), BF16),
                   jax.ShapeDtypeStruct((n_seq, RET_HEADS, RET_HEAD_DIM, RET_HEAD_DIM), F32)],
        scratch_shapes=[pltpu.VMEM((hps, RET_HEAD_DIM, RET_HEAD_DIM), F32)],
        compiler_params=_params("arbitrary", "arbitrary", "arbitrary"),
        name="retention_prompt" if state0 is None else "retention_sample",
    )(*args)


def _router_body(x_ref, g_ref, whi_ref, wlo_ref, b_ref, tri_ref, e_ref, w_ref, rank_ref, cnt_ref, carry_ref):
    ne, tb = b_ref.shape

    @pl.when(pl.program_id(0) == 0)
    def _():
        carry_ref[...] = jnp.zeros(carry_ref.shape, F32)

    hn = _rms(x_ref[...], g_ref[...])
    hi = hn.astype(BF16)
    lo = (hn - hi.astype(F32)).astype(BF16)
    dn = (((1,), (1,)), ((), ()))
    logits = (lax.dot_general(whi_ref[...], hi, dn, preferred_element_type=F32)
              + lax.dot_general(whi_ref[...], lo, dn, preferred_element_type=F32)
              + lax.dot_general(wlo_ref[...], hi, dn, preferred_element_type=F32)) + b_ref[...]
    eidx = lax.broadcasted_iota(jnp.int32, (ne, tb), 0)
    rest = logits
    tops, idxs, hots = [], [], []
    for _ in range(TOP_K):
        m = jnp.max(rest, axis=0, keepdims=True)
        idx = jnp.min(jnp.where(rest == m, eidx, ne), axis=0, keepdims=True)
        hot = eidx == idx
        tops.append(m)
        idxs.append(idx)
        hots.append(hot)
        rest = jnp.where(hot, -jnp.inf, rest)
    ex = [jnp.exp(t - tops[0]) for t in tops]
    den = ex[0] + ex[1] + ex[2] + ex[3]
    member = jnp.zeros((ne, tb), F32)
    for hot in hots:
        member = member + jnp.where(hot, 1.0, 0.0)
    before = jnp.dot(member.astype(BF16), tri_ref[...], preferred_element_type=F32)
    carry = carry_ref[...]
    base = before + jnp.concatenate([carry] * (tb // carry.shape[1]), axis=1)
    ranks = [jnp.sum(jnp.where(hot, base, 0.0), axis=0, keepdims=True) for hot in hots]
    e_ref[...] = jnp.concatenate(idxs, axis=0)
    w_ref[...] = jnp.concatenate([e / den for e in ex], axis=0)
    rank_ref[...] = jnp.concatenate(ranks, axis=0).astype(jnp.int32)
    carry = carry + jnp.sum(member, axis=1, keepdims=True)
    carry_ref[...] = carry
    cnt_ref[...] = carry.astype(jnp.int32)


def _router(x1, g, w_router, b_router):
    t, d = x1.shape
    ne = w_router.shape[1]
    tb = ROUTER_TB
    wt = w_router.T
    wt_hi = wt.astype(BF16)
    wt_lo = (wt - wt_hi.astype(F32)).astype(BF16)
    bias = jnp.broadcast_to(b_router.astype(F32)[:, None], (ne, tb))
    tri = (jnp.arange(tb)[:, None] < jnp.arange(tb)[None, :]).astype(BF16)
    const = lambda shape: pl.BlockSpec(shape, lambda i: (0,) * len(shape))
    col = pl.BlockSpec((TOP_K, tb), lambda i: (0, i))
    return pl.pallas_call(
        _router_body,
        grid=(t // tb,),
        in_specs=[pl.BlockSpec((tb, d), lambda i: (i, 0)), const((1, d)), const((ne, d)), const((ne, d)),
                  const((ne, tb)), const((tb, tb))],
        out_specs=[col, col, col, const((ne, 128))],
        out_shape=[jax.ShapeDtypeStruct((TOP_K, t), jnp.int32),
                   jax.ShapeDtypeStruct((TOP_K, t), F32),
                   jax.ShapeDtypeStruct((TOP_K, t), jnp.int32),
                   jax.ShapeDtypeStruct((ne, 128), jnp.int32)],
        scratch_shapes=[pltpu.VMEM((ne, 128), F32)],
        compiler_params=_params("arbitrary"),
        name="router",
    )(x1, g.reshape(1, d), wt_hi, wt_lo, bias, tri)


def _row_copy(src_hbm, dst, sem, src_row, dst_row):
    return pltpu.make_async_copy(src_hbm.at[pl.ds(src_row, 1), :], dst.at[pl.ds(dst_row, 1), :], sem)


def _dispatch_body(nvb_ref, tok_ref, x_hbm, g_ref, o_ref, buf_ref, sem):
    tm = buf_ref.shape[0]

    @pl.when(pl.program_id(0) < nvb_ref[0])
    def _():
        def issue(r, carry):
            _row_copy(x_hbm, buf_ref, sem, tok_ref[0, 0, r], r).start()
            return carry
        lax.fori_loop(0, tm, issue, 0)

        def drain(r, carry):
            _row_copy(x_hbm, buf_ref, sem, 0, r).wait()
            return carry
        lax.fori_loop(0, tm, drain, 0)
        o_ref[...] = _rms(buf_ref[...], g_ref[...]).astype(o_ref.dtype)

    @pl.when(pl.program_id(0) >= nvb_ref[0])
    def _():
        o_ref[...] = jnp.zeros(o_ref.shape, o_ref.dtype)


def _dispatch(x1, g, row_tok, nvb, n_blocks):
    t, d = x1.shape
    tm = MOE_TM
    last = lambda i, nvb_ref: jnp.minimum(i, nvb_ref[0] - 1)
    return pl.pallas_call(
        _dispatch_body,
        grid_spec=pltpu.PrefetchScalarGridSpec(
            num_scalar_prefetch=1,
            grid=(n_blocks,),
            in_specs=[pl.BlockSpec((1, 1, tm), lambda i, nvb_ref: (last(i, nvb_ref), 0, 0),
                                   memory_space=pltpu.SMEM),
                      pl.BlockSpec(memory_space=pl.ANY),
                      pl.BlockSpec((1, d), lambda i, nvb_ref: (0, 0))],
            out_specs=pl.BlockSpec((tm, d), lambda i, nvb_ref: (i, 0)),
            scratch_shapes=[pltpu.VMEM((tm, d), F32), pltpu.SemaphoreType.DMA(())]),
        out_shape=jax.ShapeDtypeStruct((n_blocks * tm, d), BF16),
        compiler_params=_params("arbitrary"),
        name="moe_dispatch",
    )(nvb, row_tok.reshape(n_blocks, 1, tm), x1, g.reshape(1, d))


def _new_expert(be_ref, i):
    return (i == 0) | (be_ref[i] != be_ref[jnp.maximum(i - 1, 0)])


def _gateup_body(be_ref, nvb_ref, x_ref, wg_ref, wl_ref, bg_ref, bl_ref, o_ref, wgb_ref, wlb_ref):
    i = pl.program_id(1)
    valid = i < nvb_ref[0]

    @pl.when(valid & _new_expert(be_ref, i))
    def _():
        wgb_ref[...] = wg_ref[...].astype(BF16)
        wlb_ref[...] = wl_ref[...].astype(BF16)

    @pl.when(valid)
    def _():
        x = x_ref[...]
        gate = jnp.dot(x, wgb_ref[...], preferred_element_type=F32) + bg_ref[...]
        lin = jnp.dot(x, wlb_ref[...], preferred_element_type=F32) + bl_ref[...]
        glu = jnp.minimum(gate, SWIGLU_LIMIT)
        lin = jnp.clip(lin, -SWIGLU_LIMIT, SWIGLU_LIMIT)
        o_ref[...] = (glu * _sigmoid(SWIGLU_ALPHA * glu) * (lin + 1.0)).astype(o_ref.dtype)

    @pl.when(jnp.logical_not(valid))
    def _():
        o_ref[...] = jnp.zeros(o_ref.shape, o_ref.dtype)


def _down_body(be_ref, nvb_ref, a_ref, w_ref, b_ref, o_ref, wb_ref):
    i = pl.program_id(1)
    valid = i < nvb_ref[0]

    @pl.when(valid & _new_expert(be_ref, i))
    def _():
        wb_ref[...] = w_ref[...].astype(BF16)

    @pl.when(valid)
    def _():
        o_ref[...] = jnp.dot(a_ref[...], wb_ref[...], preferred_element_type=F32) + b_ref[...]

    @pl.when(jnp.logical_not(valid))
    def _():
        o_ref[...] = jnp.zeros(o_ref.shape, o_ref.dtype)


def _blk(i, nvb_ref):
    return jnp.minimum(i, nvb_ref[0] - 1)


def _gateup(xs, w_gate_up, b_gate_up, block_e, nvb):
    rows, d = xs.shape
    ne, _, two_f = w_gate_up.shape
    f = two_f // 2
    tm, tn = MOE_TM, MOE_TN
    nt = f // tn
    wmap = lambda off: (lambda j, i, be, nv: (be[_blk(i, nv)], 0, off + j))
    return pl.pallas_call(
        _gateup_body,
        grid_spec=pltpu.PrefetchScalarGridSpec(
            num_scalar_prefetch=2,
            grid=(nt, rows // tm),
            in_specs=[pl.BlockSpec((tm, d), lambda j, i, be, nv: (_blk(i, nv), 0)),
                      pl.BlockSpec((None, d, tn), wmap(0)),
                      pl.BlockSpec((None, d, tn), wmap(nt)),
                      pl.BlockSpec((None, 1, tn), wmap(0)),
                      pl.BlockSpec((None, 1, tn), wmap(nt))],
            out_specs=pl.BlockSpec((tm, tn), lambda j, i, be, nv: (i, j)),
            scratch_shapes=[pltpu.VMEM((d, tn), BF16), pltpu.VMEM((d, tn), BF16)]),
        out_shape=jax.ShapeDtypeStruct((rows, f), BF16),
        compiler_params=_params("arbitrary", "arbitrary"),
        name="moe_gate_up",
    )(block_e, nvb, xs, w_gate_up, w_gate_up,
      b_gate_up.reshape(ne, 1, two_f), b_gate_up.reshape(ne, 1, two_f))


def _down(act, w_down, b_down, block_e, nvb):
    rows, f = act.shape
    ne, _, d = w_down.shape
    tm, tn = MOE_TM, MOE_TN
    wmap = lambda j, i, be, nv: (be[_blk(i, nv)], 0, j)
    return pl.pallas_call(
        _down_body,
        grid_spec=pltpu.PrefetchScalarGridSpec(
            num_scalar_prefetch=2,
            grid=(d // tn, rows // tm),
            in_specs=[pl.BlockSpec((tm, f), lambda j, i, be, nv: (_blk(i, nv), 0)),
                      pl.BlockSpec((None, f, tn), wmap),
                      pl.BlockSpec((None, 1, tn), wmap)],
            out_specs=pl.BlockSpec((tm, tn), lambda j, i, be, nv: (i, j)),
            scratch_shapes=[pltpu.VMEM((f, tn), BF16)]),
        out_shape=jax.ShapeDtypeStruct((rows, d), F32),
        compiler_params=_params("arbitrary", "arbitrary"),
        name="moe_down",
    )(block_e, nvb, act, w_down, b_down.reshape(ne, 1, d))


def _combine_body(dest_ref, y_hbm, x_ref, w_ref, g_ref, o_ref, buf_ref, sem):
    tb = x_ref.shape[0]
    for k in range(TOP_K):
        def issue(t, carry, k=k):
            _row_copy(y_hbm, buf_ref.at[k], sem, dest_ref[0, k, t], t).start()
            return carry
        lax.fori_loop(0, tb, issue, 0)
    for k in range(TOP_K):
        def drain(t, carry, k=k):
            _row_copy(y_hbm, buf_ref.at[k], sem, 0, t).wait()
            return carry
        lax.fori_loop(0, tb, drain, 0)
    w = w_ref[...]
    y = buf_ref[0] * w[:, 0:1]
    for k in range(1, TOP_K):
        y = y + buf_ref[k] * w[:, k:k + 1]
    o_ref[...] = _rms(x_ref[...] + y, g_ref[...])


def _combine(y_rows, x1, dest, top_w, g, tok0, n_tok):
    d = x1.shape[1]
    tb = COMBINE_TB
    blk0 = tok0 // tb
    dest_b = dest.reshape(TOP_K, -1, tb).transpose(1, 0, 2)
    return pl.pallas_call(
        _combine_body,
        grid=(n_tok // tb,),
        in_specs=[pl.BlockSpec((1, TOP_K, tb), lambda i: (blk0 + i, 0, 0), memory_space=pltpu.SMEM),
                  pl.BlockSpec(memory_space=pl.ANY),
                  pl.BlockSpec((tb, d), lambda i: (blk0 + i, 0)),
                  pl.BlockSpec((tb, TOP_K), lambda i: (blk0 + i, 0)),
                  pl.BlockSpec((1, d), lambda i: (0, 0))],
        out_specs=pl.BlockSpec((tb, d), lambda i: (i, 0)),
        out_shape=jax.ShapeDtypeStruct((n_tok, d), F32),
        scratch_shapes=[pltpu.VMEM((TOP_K, tb, d), F32), pltpu.SemaphoreType.DMA(())],
        compiler_params=_params("arbitrary"),
        name="moe_combine",
    )(dest_b, y_rows, x1, top_w.T, g.reshape(1, d))


def _plan(top_e, rank, counts, n_blocks):
    tm = MOE_TM
    n_tok = top_e.shape[1]
    padded = (counts + tm - 1) // tm * tm
    pend = jnp.cumsum(padded)
    pstart = pend - padded
    dest = pstart[top_e] + rank
    nvb = (pend[-1] // tm).astype(jnp.int32).reshape(1)
    block_e = jnp.minimum(jnp.searchsorted(pend // tm, jnp.arange(n_blocks), side="right"),
                          N_EXPERTS - 1).astype(jnp.int32)
    tok = jnp.broadcast_to(jnp.arange(n_tok, dtype=jnp.int32)[None, :], dest.shape)
    row_tok = jnp.zeros((n_blocks * tm,), jnp.int32).at[dest.reshape(-1)].set(tok.reshape(-1))
    return dest, row_tok, block_e, nvb


def kernel(x_prompt, x_sample, cache_attn_k, cache_attn_v, state_ret, norm_mix, w_in, attn_sinks,
           w_o_attn, w_o_ret, w_out, norm_ffn, w_router, b_router, w_gate_up, b_gate_up,
           w_down, b_down, norm_final):
    bp, sp, d = x_prompt.shape
    bs, ss, _ = x_sample.shape
    depth = w_in.shape[0]
    assert depth == 1 and ss == CHUNK and sp % CHUNK == 0
    tp, ts = bp * sp, bs * ss
    t = tp + ts
    ncp = sp // CHUNK

    widths = (ATTN_Q_W, ATTN_KV_W, ATTN_KV_W, RET_W, RET_W, RET_W, RET_W, d, d)
    cols = [0]
    for wdt in widths:
        cols.append(cols[-1] + wdt)
    c_aq, c_ak, c_av, c_rq, c_rk, c_rv, c_rg, c_ga, c_gr = cols[:9]

    x = jnp.concatenate([x_prompt.reshape(tp, d), x_sample.reshape(ts, d)], axis=0)
    h = _rmsnorm(x, norm_mix[0], BF16)
    z = _matmul(h, w_in[0], F32, "in_proj")

    bias = _attn_bias(attn_sinks[0])
    n_win = cache_attn_k.shape[2]
    ck = cache_attn_k[0].reshape(bs * n_win, ATTN_KV_W)
    cv = cache_attn_v[0].reshape(bs * n_win, ATTN_KV_W)
    attn_p = _attention(z, bias, 0, bp, ncp, c_ak, c_av)
    attn_s = _attention(z, bias, tp // CHUNK, bs, 1, c_ak, c_av, ck, cv)
    attn_o = jnp.concatenate([attn_p, attn_s], axis=0)

    consts = _ret_consts()
    r_p, state_p = _retention(z, consts, 0, bp, ncp, c_rq, c_rk, c_rv, c_rg)
    r_s, state_s = _retention(z, consts, tp // CHUNK, bs, 1, c_rq, c_rk, c_rv, c_rg, state_ret[0])
    r = jnp.concatenate([r_p, r_s], axis=0)

    merged = _merge(attn_o, r, w_o_attn[0], w_o_ret[0], z, c_ga, c_gr)
    x1 = _matmul(merged, w_out[0], F32, "out_proj", residual=x)

    top_e, top_w, rank, cnt = _router(x1, norm_ffn[0], w_router[0], b_router[0])
    n_blocks = -(-(t * TOP_K) // MOE_TM) + N_EXPERTS
    dest, row_tok, block_e, nvb = _plan(top_e, rank, cnt[:, 0], n_blocks)
    xs = _dispatch(x1, norm_ffn[0], row_tok, nvb, n_blocks)
    act = _gateup(xs, w_gate_up[0], b_gate_up[0], block_e, nvb)
    y_rows = _down(act, w_down[0], b_down[0], block_e, nvb)
    y_prompt = _combine(y_rows, x1, dest, top_w, norm_final, 0, tp).reshape(bp, sp, d)
    y_sample = _combine(y_rows, x1, dest, top_w, norm_final, tp, ts).reshape(bs, ss, d)

    n_keep = min(WINDOW, sp)
    zp = z[:tp].reshape(bp, sp, -1)
    zs = z[tp:].reshape(bs, ss, -1)
    kv_shape = (ATTN_KV_HEADS, ATTN_HEAD_DIM)
    new_k_p = zp[:, sp - n_keep:, c_ak:c_ak + ATTN_KV_W].reshape(1, bp, n_keep, *kv_shape)
    new_v_p = zp[:, sp - n_keep:, c_av:c_av + ATTN_KV_W].reshape(1, bp, n_keep, *kv_shape)
    new_k_s = zs[:, :, c_ak:c_ak + ATTN_KV_W].reshape(1, bs, ss, *kv_shape)
    new_v_s = zs[:, :, c_av:c_av + ATTN_KV_W].reshape(1, bs, ss, *kv_shape)
    return (y_prompt, y_sample, new_k_p, new_v_p, state_p[None], new_k_s, new_v_s, state_s[None])
```

```python
import functools

import jax
import jax.numpy as jnp
from jax import lax
from jax.experimental import pallas as pl
from jax.experimental.pallas import tpu as pltpu

F32 = jnp.float32
BF16 = jnp.bfloat16

CHUNK = 64
WINDOW = 128
ATTN_HEADS = 32
ATTN_KV_HEADS = 4
ATTN_GROUP = ATTN_HEADS // ATTN_KV_HEADS
ATTN_HEAD_DIM = 64
ATTN_Q_W = ATTN_HEADS * ATTN_HEAD_DIM
ATTN_KV_W = ATTN_KV_HEADS * ATTN_HEAD_DIM
RET_HEADS = 8
RET_HEAD_DIM = 256
RET_W = RET_HEADS * RET_HEAD_DIM
N_EXPERTS = 32
TOP_K = 4
SWIGLU_ALPHA = 1.702
SWIGLU_LIMIT = 7.0
RMS_EPS = 1e-5
NEG_INF = -1e30

ATTN_COLS = 256
RET_HEADS_PER_STEP = 2

VMEM_LIMIT_BYTES = 56 * 1024 * 1024

MOE_TM = 256
MOE_ITEM_CHUNKS = 8
MOE_UP_TN = 256
MOE_DOWN_TN = 512
PROJ_TM = 1024
PROJ_TN = 512
MERGE_TM = 512
COMBINE_TB = 128
ROUTER_TB = 256
NORM_TM = 256


def _params(*semantics):
    return pltpu.CompilerParams(dimension_semantics=semantics, vmem_limit_bytes=VMEM_LIMIT_BYTES)


def _rms(x, g):
    ms = jnp.mean(x * x, axis=-1, keepdims=True)
    return x * lax.rsqrt(ms + RMS_EPS) * g


def _sigmoid(x):
    return 1.0 / (1.0 + jnp.exp(-x))


def _first(i, nf):
    return jnp.minimum(i, nf - 1)


def _second(i, nf):
    return jnp.maximum(i - nf, 0)


def _on_rows(i, nf, fn, first_refs, second_refs):
    @pl.when(i < nf)
    def _():
        fn(*first_refs)

    @pl.when(i >= nf)
    def _():
        fn(*second_refs)


def _rmsnorm_body(xp_ref, xs_ref, g_ref, o_ref, *, nf):
    def norm(x_ref):
        o_ref[...] = _rms(x_ref[...], g_ref[...]).astype(o_ref.dtype)

    _on_rows(pl.program_id(0), nf, norm, (xp_ref,), (xs_ref,))


def _rmsnorm(xp, xs, g, out_dtype):
    (mp, d), ms = xp.shape, xs.shape[0]
    tm = min(NORM_TM, ms)
    nf = mp // tm
    return pl.pallas_call(
        functools.partial(_rmsnorm_body, nf=nf),
        grid=((mp + ms) // tm,),
        in_specs=[pl.BlockSpec((tm, d), lambda i: (_first(i, nf), 0)),
                  pl.BlockSpec((tm, d), lambda i: (_second(i, nf), 0)),
                  pl.BlockSpec((1, d), lambda i: (0, 0))],
        out_specs=pl.BlockSpec((tm, d), lambda i: (i, 0)),
        out_shape=jax.ShapeDtypeStruct((mp + ms, d), out_dtype),
        compiler_params=_params("arbitrary"),
        name="rmsnorm",
    )(xp, xs, g.reshape(1, d))


def _matmul_body(x_ref, w_ref, *rest, nf):
    if nf is None:
        o_ref, wb_ref = rest
    else:
        rp_ref, rs_ref, o_ref, wb_ref = rest
    i = pl.program_id(1)

    @pl.when(i == 0)
    def _():
        wb_ref[...] = w_ref[...].astype(BF16)

    acc = jnp.dot(x_ref[...], wb_ref[...], preferred_element_type=F32)
    if nf is None:
        o_ref[...] = acc.astype(o_ref.dtype)
    else:
        def add(r_ref):
            o_ref[...] = (acc + r_ref[...]).astype(o_ref.dtype)

        _on_rows(i, nf, add, (rp_ref,), (rs_ref,))


def _matmul(x, w, out_dtype, name, residual=None):
    m, k = x.shape
    n = w.shape[1]
    tn = min(PROJ_TN, n)
    in_specs = [None, pl.BlockSpec((k, tn), lambda j, i: (0, j))]
    args = [x, w]
    nf = None
    if residual is None:
        tm = min(PROJ_TM, m)
    else:
        tm = min(PROJ_TM, residual[1].shape[0])
        nf = residual[0].shape[0] // tm
        in_specs += [pl.BlockSpec((tm, tn), lambda j, i: (_first(i, nf), j)),
                     pl.BlockSpec((tm, tn), lambda j, i: (_second(i, nf), j))]
        args += list(residual)
    in_specs[0] = pl.BlockSpec((tm, k), lambda j, i: (i, 0))
    return pl.pallas_call(
        functools.partial(_matmul_body, nf=nf),
        grid=(n // tn, m // tm),
        in_specs=in_specs,
        out_specs=pl.BlockSpec((tm, tn), lambda j, i: (i, j)),
        out_shape=jax.ShapeDtypeStruct((m, n), out_dtype),
        scratch_shapes=[pltpu.VMEM((k, tn), BF16)],
        compiler_params=_params("arbitrary", "arbitrary"),
        name=name,
    )(*args)


def _merge_body(ap_ref, as_ref, rp_ref, rs_ref, wa_ref, wr_ref, ga_ref, gr_ref, o_ref, wab_ref, wrb_ref, *, nf):
    i = pl.program_id(1)

    @pl.when(i == 0)
    def _():
        wab_ref[...] = wa_ref[...].astype(BF16)
        wrb_ref[...] = wr_ref[...].astype(BF16)

    def merge(a_ref, r_ref):
        ya = jnp.dot(a_ref[...], wab_ref[...], preferred_element_type=F32)
        yr = jnp.dot(r_ref[...], wrb_ref[...], preferred_element_type=F32)
        o_ref[...] = (_sigmoid(ga_ref[...]) * ya + _sigmoid(gr_ref[...]) * yr).astype(o_ref.dtype)

    _on_rows(i, nf, merge, (ap_ref, rp_ref), (as_ref, rs_ref))


def _merge(a, r, w_a, w_r, z, ga_col, gr_col):
    (mp, ka), ms = a[0].shape, a[1].shape[0]
    kr = r[0].shape[1]
    n = w_a.shape[1]
    tm, tn = min(MERGE_TM, ms), min(PROJ_TN, n)
    nf = mp // tm
    ga_blk, gr_blk = ga_col // tn, gr_col // tn
    rows = lambda k, pick: pl.BlockSpec((tm, k), lambda j, i: (pick(i, nf), 0))
    return pl.pallas_call(
        functools.partial(_merge_body, nf=nf),
        grid=(n // tn, (mp + ms) // tm),
        in_specs=[rows(ka, _first), rows(ka, _second), rows(kr, _first), rows(kr, _second),
                  pl.BlockSpec((ka, tn), lambda j, i: (0, j)),
                  pl.BlockSpec((kr, tn), lambda j, i: (0, j)),
                  pl.BlockSpec((tm, tn), lambda j, i: (i, ga_blk + j)),
                  pl.BlockSpec((tm, tn), lambda j, i: (i, gr_blk + j))],
        out_specs=pl.BlockSpec((tm, tn), lambda j, i: (i, j)),
        out_shape=jax.ShapeDtypeStruct((mp + ms, n), BF16),
        scratch_shapes=[pltpu.VMEM((ka, tn), BF16), pltpu.VMEM((kr, tn), BF16)],
        compiler_params=_params("arbitrary", "arbitrary"),
        name="gated_merge",
    )(a[0], a[1], r[0], r[1], w_a, w_r, z, z)


def _attn_body(q_ref, k0_ref, k1_ref, k2_ref, v0_ref, v1_ref, v2_ref, bias_ref, o_ref, *, prompt):
    hd, grp = ATTN_HEAD_DIM, ATTN_GROUP
    rows = grp * CHUNK
    n_keys = WINDOW + CHUNK
    pad = jnp.zeros((ATTN_COLS - n_keys, hd), F32)
    if prompt:
        n_invalid = jnp.maximum(WINDOW - pl.program_id(1) * CHUNK, 0)
        col = lax.broadcasted_iota(jnp.int32, (rows, ATTN_COLS), 1)
    for h in range(ATTN_KV_HEADS):
        sl = slice(h * hd, (h + 1) * hd)
        kh = jnp.concatenate([k0_ref[:, sl], k1_ref[:, sl], k2_ref[:, sl], pad], axis=0).astype(BF16)
        vh = jnp.concatenate([v0_ref[:, sl], v1_ref[:, sl], v2_ref[:, sl], pad], axis=0).astype(BF16)
        qh = jnp.concatenate(
            [q_ref[:, (h * grp + g) * hd:(h * grp + g + 1) * hd] for g in range(grp)], axis=0).astype(BF16)
        s = lax.dot_general(qh, kh, (((1,), (1,)), ((), ())), preferred_element_type=F32)
        s = s * (hd ** -0.5) + bias_ref[h * rows:(h + 1) * rows, :]
        if prompt:
            s = jnp.where(col < n_invalid, NEG_INF, s)
        m = jnp.max(s, axis=-1, keepdims=True)
        p = jnp.exp(s - m)
        p = (p / jnp.sum(p, axis=-1, keepdims=True)).astype(BF16)
        oh = jnp.dot(p, vh, preferred_element_type=F32)
        for g in range(grp):
            off = (h * grp + g) * hd
            o_ref[:, off:off + hd] = oh[g * CHUNK:(g + 1) * CHUNK, :].astype(o_ref.dtype)


def _attn_bias(sinks):
    n_keys = WINDOW + CHUNK
    slopes = 2.0 ** (-8.0 * jnp.arange(1, ATTN_HEADS + 1, dtype=F32) / ATTN_HEADS)
    q_pos = WINDOW + jnp.arange(CHUNK)
    k_pos = jnp.arange(n_keys)
    dist = jnp.abs(q_pos[:, None] - k_pos[None, :]).astype(F32)
    alibi = -(slopes[:, None, None] * dist[None])
    sink = jnp.broadcast_to(sinks.astype(F32)[:, None, None], (ATTN_HEADS, CHUNK, 1))
    fill = jnp.full((ATTN_HEADS, CHUNK, ATTN_COLS - n_keys - 1), NEG_INF, F32)
    return jnp.concatenate([alibi, sink, fill], axis=-1).reshape(ATTN_HEADS * CHUNK, ATTN_COLS)


def _attention(z, bias, row_blk0, n_seq, n_chunks, k_col, v_col, cache_k=None, cache_v=None):
    kb, vb = k_col // ATTN_KV_W, v_col // ATTN_KV_W
    wc = WINDOW // CHUNK

    def q_map(b, c):
        return (row_blk0 + b * n_chunks + c, 0)

    if cache_k is None:
        def kv_map(j, colblk):
            return lambda b, c: (row_blk0 + b * n_chunks + jnp.maximum(c - wc + j, 0), colblk)
        kv_args = [z] * 6
        kv_specs = [pl.BlockSpec((CHUNK, ATTN_KV_W), kv_map(j, cb)) for cb in (kb, vb) for j in range(wc + 1)]
    else:
        def cache_map(j):
            return lambda b, c: (b * wc + j, 0)
        def new_map(colblk):
            return lambda b, c: (row_blk0 + b, colblk)
        kv_args = [cache_k, cache_k, z, cache_v, cache_v, z]
        kv_specs = []
        for cb in (kb, vb):
            kv_specs += [pl.BlockSpec((CHUNK, ATTN_KV_W), cache_map(j)) for j in range(wc)]
            kv_specs.append(pl.BlockSpec((CHUNK, ATTN_KV_W), new_map(cb)))
    return pl.pallas_call(
        functools.partial(_attn_body, prompt=cache_k is None),
        grid=(n_seq, n_chunks),
        in_specs=[pl.BlockSpec((CHUNK, ATTN_Q_W), q_map)] + kv_specs
                 + [pl.BlockSpec(bias.shape, lambda b, c: (0, 0))],
        out_specs=pl.BlockSpec((CHUNK, ATTN_Q_W), lambda b, c: (b * n_chunks + c, 0)),
        out_shape=jax.ShapeDtypeStruct((n_seq * n_chunks * CHUNK, ATTN_Q_W), BF16),
        compiler_params=_params("arbitrary", "arbitrary"),
        name="attn_prompt" if cache_k is None else "attn_sample",
    )(z, *kv_args, bias)


def _ret_body(*refs, has_init):
    if has_init:
        (q_ref, k_ref, v_ref, g_ref, decay_ref, dout_ref, kd_ref, g64_ref, s0_ref,
         r_ref, sout_ref, state_ref) = refs
    else:
        (q_ref, k_ref, v_ref, g_ref, decay_ref, dout_ref, kd_ref, g64_ref,
         r_ref, sout_ref, state_ref) = refs
    c = pl.program_id(2)
    hd = RET_HEAD_DIM

    @pl.when(c == 0)
    def _():
        if has_init:
            state_ref[...] = s0_ref[0]
        else:
            state_ref[...] = jnp.zeros(state_ref.shape, F32)

    for hh in range(RET_HEADS_PER_STEP):
        sl = slice(hh * hd, (hh + 1) * hd)
        q = q_ref[:, sl].astype(BF16)
        k = k_ref[:, sl] * (hd ** -0.5)
        v = v_ref[:, sl].astype(BF16)
        st = state_ref[hh]
        scores = lax.dot_general(q, k.astype(BF16), (((1,), (1,)), ((), ())),
                                 preferred_element_type=F32) * decay_ref[hh]
        o = jnp.dot(scores.astype(BF16), v, preferred_element_type=F32)
        o = o + jnp.dot(q, st.astype(BF16), preferred_element_type=F32) * dout_ref[hh]
        k_dec = (k * kd_ref[hh]).T.astype(BF16)
        state_ref[hh] = st * g64_ref[hh] + jnp.dot(k_dec, v, preferred_element_type=F32)
        rn = o * lax.rsqrt(jnp.mean(o * o, axis=-1, keepdims=True) + RMS_EPS)
        gate = g_ref[:, sl]
        r_ref[:, sl] = (rn * (gate * _sigmoid(gate))).astype(r_ref.dtype)

    @pl.when(c == pl.num_programs(2) - 1)
    def _():
        sout_ref[0] = state_ref[...]


def _ret_consts():
    log_g = jnp.log1p(-(2.0 ** (-5.0 - jnp.arange(RET_HEADS, dtype=F32))))
    idx = jnp.arange(CHUNK, dtype=F32)
    rel = idx[:, None] - idx[None, :]
    decay = jnp.where(rel >= 0, jnp.exp(log_g[:, None, None] * jnp.maximum(rel, 0.0)), 0.0)
    dout = jnp.exp((idx[None, :] + 1.0) * log_g[:, None])
    kd = jnp.exp((CHUNK - 1.0 - idx)[None, :] * log_g[:, None])
    g64 = jnp.exp(CHUNK * log_g)
    bc = lambda a: jnp.broadcast_to(a[:, :, None], (RET_HEADS, a.shape[1], RET_HEAD_DIM))
    return decay, bc(dout), bc(kd), bc(g64[:, None])


def _retention(z, consts, row_blk0, n_seq, n_chunks, q_col, k_col, v_col, g_col, state0=None):
    hps = RET_HEADS_PER_STEP
    w = hps * RET_HEAD_DIM
    n_hp = RET_HEADS // hps
    decay, dout, kd, g64 = consts

    def zmap(col):
        cb = col // w
        return lambda b, h, c: (row_blk0 + b * n_chunks + c, cb + h)

    cmap = lambda b, h, c: (h, 0, 0)
    in_specs = [pl.BlockSpec((CHUNK, w), zmap(col)) for col in (q_col, k_col, v_col, g_col)]
    in_specs += [pl.BlockSpec((hps, CHUNK, CHUNK), cmap),
                 pl.BlockSpec((hps, CHUNK, RET_HEAD_DIM), cmap),
                 pl.BlockSpec((hps, CHUNK, RET_HEAD_DIM), cmap),
                 pl.BlockSpec((hps, 1, RET_HEAD_DIM), cmap)]
    args = [z, z, z, z, decay, dout, kd, g64]
    st_spec = pl.BlockSpec((1, hps, RET_HEAD_DIM, RET_HEAD_DIM), lambda b, h, c: (b, h, 0, 0))
    if state0 is not None:
        in_specs.append(st_spec)
        args.append(state0)
    return pl.pallas_call(
        functools.partial(_ret_body, has_init=state0 is not None),
        grid=(n_seq, n_hp, n_chunks),
        in_specs=in_specs,
        out_specs=[pl.BlockSpec((CHUNK, w), lambda b, h, c: (b * n_chunks + c, h)), st_spec],
        out_shape=[jax.ShapeDtypeStruct((n_seq * n_chunks * CHUNK, RET_W), BF16),
                   jax.ShapeDtypeStruct((n_seq, RET_HEADS, RET_HEAD_DIM, RET_HEAD_DIM), F32)],
        scratch_shapes=[pltpu.VMEM((hps, RET_HEAD_DIM, RET_HEAD_DIM), F32)],
        compiler_params=_params("arbitrary", "arbitrary", "arbitrary"),
        name="retention_prompt" if state0 is None else "retention_sample",
    )(*args)


def _router_body(x_ref, g_ref, whi_ref, wlo_ref, b_ref, tri_ref, e_ref, w_ref, rank_ref, cnt_ref, carry_ref):
    ne, tb = b_ref.shape

    @pl.when(pl.program_id(0) == 0)
    def _():
        carry_ref[...] = jnp.zeros(carry_ref.shape, F32)

    hn = _rms(x_ref[...], g_ref[...])
    hi = hn.astype(BF16)
    lo = (hn - hi.astype(F32)).astype(BF16)
    dn = (((1,), (1,)), ((), ()))
    logits = (lax.dot_general(whi_ref[...], hi, dn, preferred_element_type=F32)
              + lax.dot_general(whi_ref[...], lo, dn, preferred_element_type=F32)
              + lax.dot_general(wlo_ref[...], hi, dn, preferred_element_type=F32)) + b_ref[...]
    eidx = lax.broadcasted_iota(jnp.int32, (ne, tb), 0)
    rest = logits
    tops, idxs, hots = [], [], []
    for _ in range(TOP_K):
        m = jnp.max(rest, axis=0, keepdims=True)
        idx = jnp.min(jnp.where(rest == m, eidx, ne), axis=0, keepdims=True)
        hot = eidx == idx
        tops.append(m)
        idxs.append(idx)
        hots.append(hot)
        rest = jnp.where(hot, -jnp.inf, rest)
    ex = [jnp.exp(t - tops[0]) for t in tops]
    den = ex[0] + ex[1] + ex[2] + ex[3]
    member = jnp.zeros((ne, tb), F32)
    for hot in hots:
        member = member + jnp.where(hot, 1.0, 0.0)
    before = jnp.dot(member.astype(BF16), tri_ref[...], preferred_element_type=F32)
    carry = carry_ref[...]
    base = before + jnp.concatenate([carry] * (tb // carry.shape[1]), axis=1)
    ranks = [jnp.sum(jnp.where(hot, base, 0.0), axis=0, keepdims=True) for hot in hots]
    e_ref[...] = jnp.concatenate(idxs, axis=0)
    w_ref[...] = jnp.concatenate([e / den for e in ex], axis=0)
    rank_ref[...] = jnp.concatenate(ranks, axis=0).astype(jnp.int32)
    carry = carry + jnp.sum(member, axis=1, keepdims=True)
    carry_ref[...] = carry
    cnt_ref[...] = carry.astype(jnp.int32)


def _router(x1, g, w_router, b_router):
    t, d = x1.shape
    ne = w_router.shape[1]
    tb = ROUTER_TB
    wt = w_router.T
    wt_hi = wt.astype(BF16)
    wt_lo = (wt - wt_hi.astype(F32)).astype(BF16)
    bias = jnp.broadcast_to(b_router.astype(F32)[:, None], (ne, tb))
    tri = (jnp.arange(tb)[:, None] < jnp.arange(tb)[None, :]).astype(BF16)
    const = lambda shape: pl.BlockSpec(shape, lambda i: (0,) * len(shape))
    col = pl.BlockSpec((TOP_K, tb), lambda i: (0, i))
    return pl.pallas_call(
        _router_body,
        grid=(t // tb,),
        in_specs=[pl.BlockSpec((tb, d), lambda i: (i, 0)), const((1, d)), const((ne, d)), const((ne, d)),
                  const((ne, tb)), const((tb, tb))],
        out_specs=[col, col, col, const((ne, 128))],
        out_shape=[jax.ShapeDtypeStruct((TOP_K, t), jnp.int32),
                   jax.ShapeDtypeStruct((TOP_K, t), F32),
                   jax.ShapeDtypeStruct((TOP_K, t), jnp.int32),
                   jax.ShapeDtypeStruct((ne, 128), jnp.int32)],
        scratch_shapes=[pltpu.VMEM((ne, 128), F32)],
        compiler_params=_params("arbitrary"),
        name="router",
    )(x1, g.reshape(1, d), wt_hi, wt_lo, bias, tri)


DMA_ISSUE_UNROLL = 8
ROW_GROUP = 32


def _row_copy(src_hbm, dst, sem, src_row, dst_row):
    return pltpu.make_async_copy(src_hbm.at[pl.ds(src_row, 1), :], dst.at[pl.ds(dst_row, 1), :], sem)


def _rows_wait(src_hbm, dst, sem):
    pltpu.make_async_copy(src_hbm.at[pl.ds(0, dst.shape[0]), :], dst, sem).wait()


def _dispatch_body(nvb_ref, tok_ref, tok_next_ref, x_hbm, g_ref, o_ref, buf_ref, sem):
    i = pl.program_id(0)
    nvb = nvb_ref[0]
    tm = buf_ref.shape[1]

    def issue(idx_ref, slot):
        def body(r, carry):
            _row_copy(x_hbm, buf_ref.at[slot], sem.at[slot], idx_ref[0, 0, r], r).start()
            return carry
        lax.fori_loop(0, tm, body, 0, unroll=DMA_ISSUE_UNROLL)

    @pl.when(i == 0)
    def _():
        issue(tok_ref, 0)

    @pl.when(i + 1 < nvb)
    def _():
        issue(tok_next_ref, (i + 1) % 2)

    @pl.when(i < nvb)
    def _():
        slot = i % 2
        _rows_wait(x_hbm, buf_ref.at[slot], sem.at[slot])

        def group(c, carry):
            sl = pl.ds(pl.multiple_of(c * ROW_GROUP, ROW_GROUP), ROW_GROUP)
            o_ref[sl, :] = _rms(buf_ref[slot, sl, :], g_ref[...]).astype(o_ref.dtype)
            return carry
        lax.fori_loop(0, tm // ROW_GROUP, group, 0)

    @pl.when(i >= nvb)
    def _():
        o_ref[...] = jnp.zeros(o_ref.shape, o_ref.dtype)


def _dispatch(x1, g, row_tok, nvb, n_blocks):
    t, d = x1.shape
    tm = MOE_TM
    tok_spec = lambda ahead: pl.BlockSpec(
        (1, 1, tm), lambda i, nvb_ref: (jnp.minimum(i + ahead, nvb_ref[0] - 1), 0, 0), memory_space=pltpu.SMEM)
    row_tok = row_tok.reshape(n_blocks, 1, tm)
    return pl.pallas_call(
        _dispatch_body,
        grid_spec=pltpu.PrefetchScalarGridSpec(
            num_scalar_prefetch=1,
            grid=(n_blocks,),
            in_specs=[tok_spec(0), tok_spec(1),
                      pl.BlockSpec(memory_space=pl.ANY),
                      pl.BlockSpec((1, d), lambda i, nvb_ref: (0, 0))],
            out_specs=pl.BlockSpec((tm, d), lambda i, nvb_ref: (i, 0)),
            scratch_shapes=[pltpu.VMEM((2, tm, d), F32), pltpu.SemaphoreType.DMA((2,))]),
        out_shape=jax.ShapeDtypeStruct((n_blocks * tm, d), BF16),
        compiler_params=_params("arbitrary"),
        name="moe_dispatch",
    )(nvb, row_tok, row_tok, x1, g.reshape(1, d))


def _for_chunks(nch, fn):
    def body(c, carry):
        fn(c)
        return carry
    lax.fori_loop(0, nch, body, 0)


def _chunk_rows(base, c):
    return pl.ds(pl.multiple_of(base + c * MOE_TM, MOE_TM), MOE_TM)


def _stream_out(nch, compute, produce, out_copy, stage_ref):
    def step(value_of):
        def chunk(c):
            slot = c % 2

            @pl.when(c >= 2)
            def _():
                out_copy(c - 2, slot).wait()

            stage_ref[slot] = value_of(c)
            out_copy(c, slot).start()
        return chunk

    @pl.when(compute)
    def _():
        _for_chunks(nch, step(produce))

    @pl.when(jnp.logical_not(compute))
    def _():
        _for_chunks(nch, step(lambda c: jnp.zeros(stage_ref.shape[1:], stage_ref.dtype)))

    @pl.when(nch >= 2)
    def _():
        out_copy(nch - 2, nch % 2).wait()

    @pl.when(nch >= 1)
    def _():
        out_copy(nch - 1, (nch - 1) % 2).wait()


def _gateup_body(ie_ref, row0_ref, nch_ref, kind_ref, x_hbm, wg_ref, wl_ref, bg_ref, bl_ref, act_hbm,
                 xbuf_ref, wb_ref, stage_ref, sem_x, sem_o):
    t, j = pl.program_id(0), pl.program_id(1)
    nch, r0 = nch_ref[t], row0_ref[t]
    compute = kind_ref[t] == 0
    tn = wg_ref.shape[1]

    def x_copy(c):
        return pltpu.make_async_copy(x_hbm.at[_chunk_rows(r0, c), :], xbuf_ref.at[_chunk_rows(0, c), :], sem_x)

    def out_copy(c, slot):
        return pltpu.make_async_copy(stage_ref.at[slot], act_hbm.at[j, _chunk_rows(r0, c), :], sem_o.at[slot])

    @pl.when(compute & (j == 0))
    def _():
        _for_chunks(nch, lambda c: x_copy(c).start())
        _for_chunks(nch, lambda c: x_copy(c).wait())

    @pl.when(compute & (nch > 0))
    def _():
        wb_ref[:, :tn] = wg_ref[...].astype(BF16)
        wb_ref[:, tn:] = wl_ref[...].astype(BF16)

    def produce(c):
        gu = jnp.dot(xbuf_ref[_chunk_rows(0, c), :], wb_ref[...], preferred_element_type=F32)
        glu = jnp.minimum(gu[:, :tn] + bg_ref[...], SWIGLU_LIMIT)
        lin = jnp.clip(gu[:, tn:] + bl_ref[...], -SWIGLU_LIMIT, SWIGLU_LIMIT)
        return (glu * _sigmoid(SWIGLU_ALPHA * glu) * (lin + 1.0)).astype(stage_ref.dtype)

    _stream_out(nch, compute, produce, out_copy, stage_ref)


def _down_body(ie_ref, row0_ref, nch_ref, kind_ref, a_hbm, w_ref, b_ref, y_hbm,
               abuf_ref, wb_ref, stage_ref, sem_a, sem_o):
    t, j = pl.program_id(0), pl.program_id(1)
    nch, r0 = nch_ref[t], row0_ref[t]
    compute = kind_ref[t] == 0
    n_src, _, src_w = a_hbm.shape
    tn = w_ref.shape[1]

    def a_copy(c, s):
        return pltpu.make_async_copy(a_hbm.at[s, _chunk_rows(r0, c), :],
                                     abuf_ref.at[_chunk_rows(0, c), pl.ds(s * src_w, src_w)], sem_a)

    def out_copy(c, slot):
        cols = pl.ds(pl.multiple_of(j * tn, tn), tn)
        return pltpu.make_async_copy(stage_ref.at[slot], y_hbm.at[_chunk_rows(r0, c), cols], sem_o.at[slot])

    @pl.when(compute & (j == 0))
    def _():
        def start(c):
            for s in range(n_src):
                a_copy(c, s).start()

        def wait(c):
            for s in range(n_src):
                a_copy(c, s).wait()

        _for_chunks(nch, start)
        _for_chunks(nch, wait)

    @pl.when(compute & (nch > 0))
    def _():
        wb_ref[...] = w_ref[...].astype(BF16)

    def produce(c):
        return jnp.dot(abuf_ref[_chunk_rows(0, c), :], wb_ref[...], preferred_element_type=F32) + b_ref[...]

    _stream_out(nch, compute, produce, out_copy, stage_ref)


def _item_specs(n_tiles):
    def tile(j, t, nch, kind):
        return jnp.where((kind[t] == 0) & (nch[t] > 0), j, n_tiles - 1)
    return lambda off: (lambda t, j, ie, row0, nch, kind: (ie[t], 0, off + tile(j, t, nch, kind)))


def _gateup(xs, w_gate_up, b_gate_up, items):
    rows, d = xs.shape
    ne, _, two_f = w_gate_up.shape
    f = two_f // 2
    tn = MOE_UP_TN
    nt = f // tn
    wmap = _item_specs(nt)
    return pl.pallas_call(
        _gateup_body,
        grid_spec=pltpu.PrefetchScalarGridSpec(
            num_scalar_prefetch=4,
            grid=(items[0].shape[0], nt),
            in_specs=[pl.BlockSpec(memory_space=pl.ANY),
                      pl.BlockSpec((None, d, tn), wmap(0)),
                      pl.BlockSpec((None, d, tn), wmap(nt)),
                      pl.BlockSpec((None, 1, tn), wmap(0)),
                      pl.BlockSpec((None, 1, tn), wmap(nt))],
            out_specs=pl.BlockSpec(memory_space=pl.ANY),
            scratch_shapes=[pltpu.VMEM((MOE_ITEM_CHUNKS * MOE_TM, d), BF16),
                            pltpu.VMEM((d, 2 * tn), BF16),
                            pltpu.VMEM((2, MOE_TM, tn), BF16),
                            pltpu.SemaphoreType.DMA(()),
                            pltpu.SemaphoreType.DMA((2,))]),
        out_shape=jax.ShapeDtypeStruct((nt, rows, tn), BF16),
        compiler_params=_params("arbitrary", "arbitrary"),
        name="moe_gate_up",
    )(*items, xs, w_gate_up, w_gate_up,
      b_gate_up.reshape(ne, 1, two_f), b_gate_up.reshape(ne, 1, two_f))


def _down(act, w_down, b_down, items):
    n_src, rows, src_w = act.shape
    ne, f, d = w_down.shape
    tn = MOE_DOWN_TN
    nt = d // tn
    wmap = _item_specs(nt)
    return pl.pallas_call(
        _down_body,
        grid_spec=pltpu.PrefetchScalarGridSpec(
            num_scalar_prefetch=4,
            grid=(items[0].shape[0], nt),
            in_specs=[pl.BlockSpec(memory_space=pl.ANY),
                      pl.BlockSpec((None, f, tn), wmap(0)),
                      pl.BlockSpec((None, 1, tn), wmap(0))],
            out_specs=pl.BlockSpec(memory_space=pl.ANY),
            scratch_shapes=[pltpu.VMEM((MOE_ITEM_CHUNKS * MOE_TM, f), BF16),
                            pltpu.VMEM((f, tn), BF16),
                            pltpu.VMEM((2, MOE_TM, tn), F32),
                            pltpu.SemaphoreType.DMA(()),
                            pltpu.SemaphoreType.DMA((2,))]),
        out_shape=jax.ShapeDtypeStruct((rows, d), F32),
        compiler_params=_params("arbitrary", "arbitrary"),
        name="moe_down",
    )(*items, act, w_down, b_down.reshape(ne, 1, d))


def _combine_body(dest_ref, dest_next_ref, y_hbm, x_ref, w_ref, g_ref, o_ref, buf_ref, sem, *, n_steps):
    i = pl.program_id(0)
    tb = x_ref.shape[0]

    def issue(idx_ref, slot):
        for k in range(TOP_K):
            def body(t, carry, k=k):
                _row_copy(y_hbm, buf_ref.at[slot, k], sem.at[slot], idx_ref[0, k, t], t).start()
                return carry
            lax.fori_loop(0, tb, body, 0, unroll=DMA_ISSUE_UNROLL)

    @pl.when(i == 0)
    def _():
        issue(dest_ref, 0)

    @pl.when(i + 1 < n_steps)
    def _():
        issue(dest_next_ref, (i + 1) % 2)

    slot = i % 2
    for k in range(TOP_K):
        _rows_wait(y_hbm, buf_ref.at[slot, k], sem.at[slot])

    def group(c, carry):
        sl = pl.ds(pl.multiple_of(c * ROW_GROUP, ROW_GROUP), ROW_GROUP)
        w = w_ref[sl, :]
        y = buf_ref[slot, 0, sl, :] * w[:, 0:1]
        for k in range(1, TOP_K):
            y = y + buf_ref[slot, k, sl, :] * w[:, k:k + 1]
        o_ref[sl, :] = _rms(x_ref[sl, :] + y, g_ref[...])
        return carry
    lax.fori_loop(0, tb // ROW_GROUP, group, 0)


def _combine(y_rows, x1, dest, top_w, g, tok0, n_tok):
    d = x1.shape[1]
    tb = COMBINE_TB
    blk0, nb = tok0 // tb, n_tok // tb
    dest_b = dest.reshape(TOP_K, -1, tb).transpose(1, 0, 2)
    dest_spec = lambda ahead: pl.BlockSpec(
        (1, TOP_K, tb), lambda i: (blk0 + jnp.minimum(i + ahead, nb - 1), 0, 0), memory_space=pltpu.SMEM)
    return pl.pallas_call(
        functools.partial(_combine_body, n_steps=nb),
        grid=(nb,),
        in_specs=[dest_spec(0), dest_spec(1),
                  pl.BlockSpec(memory_space=pl.ANY),
                  pl.BlockSpec((tb, d), lambda i: (blk0 + i, 0)),
                  pl.BlockSpec((tb, TOP_K), lambda i: (blk0 + i, 0)),
                  pl.BlockSpec((1, d), lambda i: (0, 0))],
        out_specs=pl.BlockSpec((tb, d), lambda i: (i, 0)),
        out_shape=jax.ShapeDtypeStruct((n_tok, d), F32),
        scratch_shapes=[pltpu.VMEM((2, TOP_K, tb, d), F32), pltpu.SemaphoreType.DMA((2,))],
        compiler_params=_params("arbitrary"),
        name="moe_combine_prompt" if tok0 == 0 else "moe_combine_sample",
    )(dest_b, dest_b, y_rows, x1, top_w.T, g.reshape(1, d))


def _pick(table, idx):
    hot = idx[..., None] == jnp.arange(table.shape[0], dtype=idx.dtype)
    return jnp.sum(jnp.where(hot, table, 0), axis=-1)


def _plan(top_e, rank, counts, n_blocks, n_items):
    tm, cpi = MOE_TM, MOE_ITEM_CHUNKS
    n_tok = top_e.shape[1]
    i32 = jnp.int32
    chunks = (counts + tm - 1) // tm
    cend = jnp.cumsum(chunks)
    cstart = cend - chunks
    nvb = cend[-1]
    dest = _pick(cstart * tm, top_e) + rank
    tok = jnp.broadcast_to(jnp.arange(n_tok, dtype=i32)[None, :], dest.shape)
    row_tok = jnp.zeros((n_blocks * tm,), i32).at[dest.reshape(-1)].set(tok.reshape(-1))

    per_e = (chunks + cpi - 1) // cpi
    iend = jnp.cumsum(per_e)
    n_compute = iend[-1]
    t = jnp.arange(n_items, dtype=i32)
    is_c = t < n_compute
    e_t = jnp.minimum(jnp.sum(jnp.minimum(t, n_compute - 1)[:, None] >= iend[None, :], axis=1), N_EXPERTS - 1).astype(i32)
    k_t = t - _pick(iend - per_e, e_t)
    z_t = t - n_compute
    row0 = jnp.where(is_c, (_pick(cstart, e_t) + k_t * cpi) * tm, (nvb + z_t * cpi) * tm)
    nch = jnp.where(is_c, _pick(chunks, e_t) - k_t * cpi, (n_blocks - nvb) - z_t * cpi)
    nch = jnp.clip(nch, 0, cpi)
    row0 = jnp.where(nch > 0, row0, 0)
    kind = jnp.where(is_c, 0, 1)
    items = tuple(a.astype(i32) for a in (e_t, row0, nch, kind))
    return dest, row_tok, nvb.astype(i32).reshape(1), items


def kernel(x_prompt, x_sample, cache_attn_k, cache_attn_v, state_ret, norm_mix, w_in, attn_sinks,
           w_o_attn, w_o_ret, w_out, norm_ffn, w_router, b_router, w_gate_up, b_gate_up,
           w_down, b_down, norm_final):
    bp, sp, d = x_prompt.shape
    bs, ss, _ = x_sample.shape
    depth = w_in.shape[0]
    assert depth == 1 and ss == CHUNK and sp % CHUNK == 0
    tp, ts = bp * sp, bs * ss
    t = tp + ts
    ncp = sp // CHUNK

    widths = (ATTN_Q_W, ATTN_KV_W, ATTN_KV_W, RET_W, RET_W, RET_W, RET_W, d, d)
    cols = [0]
    for wdt in widths:
        cols.append(cols[-1] + wdt)
    c_aq, c_ak, c_av, c_rq, c_rk, c_rv, c_rg, c_ga, c_gr = cols[:9]

    x = (x_prompt.reshape(tp, d), x_sample.reshape(ts, d))
    h = _rmsnorm(x[0], x[1], norm_mix[0], BF16)
    z = _matmul(h, w_in[0], F32, "in_proj")

    bias = _attn_bias(attn_sinks[0])
    n_win = cache_attn_k.shape[2]
    ck = cache_attn_k[0].reshape(bs * n_win, ATTN_KV_W)
    cv = cache_attn_v[0].reshape(bs * n_win, ATTN_KV_W)
    attn_p = _attention(z, bias, 0, bp, ncp, c_ak, c_av)
    attn_s = _attention(z, bias, tp // CHUNK, bs, 1, c_ak, c_av, ck, cv)

    consts = _ret_consts()
    r_p, state_p = _retention(z, consts, 0, bp, ncp, c_rq, c_rk, c_rv, c_rg)
    r_s, state_s = _retention(z, consts, tp // CHUNK, bs, 1, c_rq, c_rk, c_rv, c_rg, state_ret[0])

    merged = _merge((attn_p, attn_s), (r_p, r_s), w_o_attn[0], w_o_ret[0], z, c_ga, c_gr)
    x1 = _matmul(merged, w_out[0], F32, "out_proj", residual=x)

    top_e, top_w, rank, cnt = _router(x1, norm_ffn[0], w_router[0], b_router[0])
    n_blocks = -(-(t * TOP_K) // MOE_TM) + N_EXPERTS
    n_items = N_EXPERTS + n_blocks // MOE_ITEM_CHUNKS + 2
    dest, row_tok, nvb, items = _plan(top_e, rank, cnt[:, 0], n_blocks, n_items)
    xs = _dispatch(x1, norm_ffn[0], row_tok, nvb, n_blocks)
    act = _gateup(xs, w_gate_up[0], b_gate_up[0], items)
    y_rows = _down(act, w_down[0], b_down[0], items)
    y_prompt = _combine(y_rows, x1, dest, top_w, norm_final, 0, tp).reshape(bp, sp, d)
    y_sample = _combine(y_rows, x1, dest, top_w, norm_final, tp, ts).reshape(bs, ss, d)

    n_keep = min(WINDOW, sp)
    kv = z[:, c_ak:c_ak + 2 * ATTN_KV_W]
    kv_p = kv[:tp].reshape(bp, sp, 2, ATTN_KV_HEADS, ATTN_HEAD_DIM)[:, sp - n_keep:]
    kv_s = kv[tp:].reshape(bs, ss, 2, ATTN_KV_HEADS, ATTN_HEAD_DIM)
    return (y_prompt, y_sample, kv_p[:, :, 0][None], kv_p[:, :, 1][None], state_p[None],
            kv_s[:, :, 0][None], kv_s[:, :, 1][None], state_s[None])
```

```python
import functools

import jax
import jax.numpy as jnp
from jax import lax
from jax.experimental import pallas as pl
from jax.experimental.pallas import tpu as pltpu

F32 = jnp.float32
BF16 = jnp.bfloat16

CHUNK = 64
WINDOW = 128
ATTN_HEADS = 32
ATTN_KV_HEADS = 4
ATTN_GROUP = ATTN_HEADS // ATTN_KV_HEADS
ATTN_HEAD_DIM = 64
ATTN_Q_W = ATTN_HEADS * ATTN_HEAD_DIM
ATTN_KV_W = ATTN_KV_HEADS * ATTN_HEAD_DIM
RET_HEADS = 8
RET_HEAD_DIM = 256
RET_W = RET_HEADS * RET_HEAD_DIM
N_EXPERTS = 32
TOP_K = 4
SWIGLU_ALPHA = 1.702
SWIGLU_LIMIT = 7.0
RMS_EPS = 1e-5
NEG_INF = -1e30

ATTN_COLS = 256
RET_HEADS_PER_STEP = 2
RET_SEQS_PER_STEP = 4

VMEM_LIMIT_BYTES = 56 * 1024 * 1024

MOE_TM = 128
MOE_ITEM_UNITS = 16
DISPATCH_TM = 256
MOE_UP_TN = 256
MOE_DOWN_TN = 512
PROJ_TM = 1024
PROJ_TN = 512
MERGE_TM = 512
COMBINE_TB = 128
ROUTER_TB = 256
NORM_TM = 256


def _params(*semantics):
    return pltpu.CompilerParams(dimension_semantics=semantics, vmem_limit_bytes=VMEM_LIMIT_BYTES)


def _rms(x, g):
    ms = jnp.mean(x * x, axis=-1, keepdims=True)
    return x * lax.rsqrt(ms + RMS_EPS) * g


def _sigmoid(x):
    return 1.0 / (1.0 + jnp.exp(-x))


def _first(i, nf):
    return jnp.minimum(i, nf - 1)


def _second(i, nf):
    return jnp.maximum(i - nf, 0)


def _on_rows(i, nf, fn, first_refs, second_refs):
    @pl.when(i < nf)
    def _():
        fn(*first_refs)

    @pl.when(i >= nf)
    def _():
        fn(*second_refs)


def _rmsnorm_body(xp_ref, xs_ref, g_ref, o_ref, *, nf):
    def norm(x_ref):
        o_ref[...] = _rms(x_ref[...], g_ref[...]).astype(o_ref.dtype)

    _on_rows(pl.program_id(0), nf, norm, (xp_ref,), (xs_ref,))


def _rmsnorm(xp, xs, g, out_dtype):
    (mp, d), ms = xp.shape, xs.shape[0]
    tm = min(NORM_TM, ms)
    nf = mp // tm
    return pl.pallas_call(
        functools.partial(_rmsnorm_body, nf=nf),
        grid=((mp + ms) // tm,),
        in_specs=[pl.BlockSpec((tm, d), lambda i: (_first(i, nf), 0)),
                  pl.BlockSpec((tm, d), lambda i: (_second(i, nf), 0)),
                  pl.BlockSpec((1, d), lambda i: (0, 0))],
        out_specs=pl.BlockSpec((tm, d), lambda i: (i, 0)),
        out_shape=jax.ShapeDtypeStruct((mp + ms, d), out_dtype),
        compiler_params=_params("arbitrary"),
        name="rmsnorm",
    )(xp, xs, g.reshape(1, d))


def _matmul_body(x_ref, w_ref, *rest, nf):
    if nf is None:
        o_ref, wb_ref = rest
    else:
        rp_ref, rs_ref, o_ref, wb_ref = rest
    i = pl.program_id(1)

    @pl.when(i == 0)
    def _():
        wb_ref[...] = w_ref[...].astype(BF16)

    acc = jnp.dot(x_ref[...], wb_ref[...], preferred_element_type=F32)
    if nf is None:
        o_ref[...] = acc.astype(o_ref.dtype)
    else:
        def add(r_ref):
            o_ref[...] = (acc + r_ref[...]).astype(o_ref.dtype)

        _on_rows(i, nf, add, (rp_ref,), (rs_ref,))


def _matmul(x, w, out_dtype, name, residual=None):
    m, k = x.shape
    n = w.shape[1]
    tn = min(PROJ_TN, n)
    in_specs = [None, pl.BlockSpec((k, tn), lambda j, i: (0, j))]
    args = [x, w]
    nf = None
    if residual is None:
        tm = min(PROJ_TM, m)
    else:
        tm = min(PROJ_TM, residual[1].shape[0])
        nf = residual[0].shape[0] // tm
        in_specs += [pl.BlockSpec((tm, tn), lambda j, i: (_first(i, nf), j)),
                     pl.BlockSpec((tm, tn), lambda j, i: (_second(i, nf), j))]
        args += list(residual)
    in_specs[0] = pl.BlockSpec((tm, k), lambda j, i: (i, 0))
    return pl.pallas_call(
        functools.partial(_matmul_body, nf=nf),
        grid=(n // tn, m // tm),
        in_specs=in_specs,
        out_specs=pl.BlockSpec((tm, tn), lambda j, i: (i, j)),
        out_shape=jax.ShapeDtypeStruct((m, n), out_dtype),
        scratch_shapes=[pltpu.VMEM((k, tn), BF16)],
        compiler_params=_params("arbitrary", "arbitrary"),
        name=name,
    )(*args)


def _merge_body(ap_ref, as_ref, rp_ref, rs_ref, wa_ref, wr_ref, ga_ref, gr_ref, o_ref, wab_ref, wrb_ref, *, nf):
    i = pl.program_id(1)

    @pl.when(i == 0)
    def _():
        wab_ref[...] = wa_ref[...].astype(BF16)
        wrb_ref[...] = wr_ref[...].astype(BF16)

    def merge(a_ref, r_ref):
        ya = jnp.dot(a_ref[...], wab_ref[...], preferred_element_type=F32)
        yr = jnp.dot(r_ref[...], wrb_ref[...], preferred_element_type=F32)
        o_ref[...] = (_sigmoid(ga_ref[...]) * ya + _sigmoid(gr_ref[...]) * yr).astype(o_ref.dtype)

    _on_rows(i, nf, merge, (ap_ref, rp_ref), (as_ref, rs_ref))


def _merge(a, r, w_a, w_r, z, ga_col, gr_col):
    (mp, ka), ms = a[0].shape, a[1].shape[0]
    kr = r[0].shape[1]
    n = w_a.shape[1]
    tm, tn = min(MERGE_TM, ms), min(PROJ_TN, n)
    nf = mp // tm
    ga_blk, gr_blk = ga_col // tn, gr_col // tn
    rows = lambda k, pick: pl.BlockSpec((tm, k), lambda j, i: (pick(i, nf), 0))
    return pl.pallas_call(
        functools.partial(_merge_body, nf=nf),
        grid=(n // tn, (mp + ms) // tm),
        in_specs=[rows(ka, _first), rows(ka, _second), rows(kr, _first), rows(kr, _second),
                  pl.BlockSpec((ka, tn), lambda j, i: (0, j)),
                  pl.BlockSpec((kr, tn), lambda j, i: (0, j)),
                  pl.BlockSpec((tm, tn), lambda j, i: (i, ga_blk + j)),
                  pl.BlockSpec((tm, tn), lambda j, i: (i, gr_blk + j))],
        out_specs=pl.BlockSpec((tm, tn), lambda j, i: (i, j)),
        out_shape=jax.ShapeDtypeStruct((mp + ms, n), BF16),
        scratch_shapes=[pltpu.VMEM((ka, tn), BF16), pltpu.VMEM((kr, tn), BF16)],
        compiler_params=_params("arbitrary", "arbitrary"),
        name="gated_merge",
    )(a[0], a[1], r[0], r[1], w_a, w_r, z, z)


def _attn_body(q_ref, k0_ref, k1_ref, k2_ref, v0_ref, v1_ref, v2_ref, bias_ref, o_ref, *, prompt):
    hd, grp = ATTN_HEAD_DIM, ATTN_GROUP
    rows = grp * CHUNK
    n_keys = WINDOW + CHUNK
    pad = jnp.zeros((ATTN_COLS - n_keys, hd), F32)
    if prompt:
        n_invalid = jnp.maximum(WINDOW - pl.program_id(1) * CHUNK, 0)
        col = lax.broadcasted_iota(jnp.int32, (rows, ATTN_COLS), 1)
    for h in range(ATTN_KV_HEADS):
        sl = slice(h * hd, (h + 1) * hd)
        kh = jnp.concatenate([k0_ref[:, sl], k1_ref[:, sl], k2_ref[:, sl], pad], axis=0).astype(BF16)
        vh = jnp.concatenate([v0_ref[:, sl], v1_ref[:, sl], v2_ref[:, sl], pad], axis=0).astype(BF16)
        qh = jnp.concatenate(
            [q_ref[:, (h * grp + g) * hd:(h * grp + g + 1) * hd] for g in range(grp)], axis=0).astype(BF16)
        s = lax.dot_general(qh, kh, (((1,), (1,)), ((), ())), preferred_element_type=F32)
        s = s * (hd ** -0.5) + bias_ref[h * rows:(h + 1) * rows, :]
        if prompt:
            s = jnp.where(col < n_invalid, NEG_INF, s)
        m = jnp.max(s, axis=-1, keepdims=True)
        p = jnp.exp(s - m)
        p = (p / jnp.sum(p, axis=-1, keepdims=True)).astype(BF16)
        oh = jnp.dot(p, vh, preferred_element_type=F32)
        for g in range(grp):
            off = (h * grp + g) * hd
            o_ref[:, off:off + hd] = oh[g * CHUNK:(g + 1) * CHUNK, :].astype(o_ref.dtype)


def _attn_bias(sinks):
    n_keys = WINDOW + CHUNK
    slopes = 2.0 ** (-8.0 * jnp.arange(1, ATTN_HEADS + 1, dtype=F32) / ATTN_HEADS)
    q_pos = WINDOW + jnp.arange(CHUNK)
    k_pos = jnp.arange(n_keys)
    dist = jnp.abs(q_pos[:, None] - k_pos[None, :]).astype(F32)
    alibi = -(slopes[:, None, None] * dist[None])
    sink = jnp.broadcast_to(sinks.astype(F32)[:, None, None], (ATTN_HEADS, CHUNK, 1))
    fill = jnp.full((ATTN_HEADS, CHUNK, ATTN_COLS - n_keys - 1), NEG_INF, F32)
    return jnp.concatenate([alibi, sink, fill], axis=-1).reshape(ATTN_HEADS * CHUNK, ATTN_COLS)


def _attention(z, bias, row_blk0, n_seq, n_chunks, k_col, v_col, cache_k=None, cache_v=None):
    kb, vb = k_col // ATTN_KV_W, v_col // ATTN_KV_W
    wc = WINDOW // CHUNK

    def q_map(b, c):
        return (row_blk0 + b * n_chunks + c, 0)

    if cache_k is None:
        def kv_map(j, colblk):
            return lambda b, c: (row_blk0 + b * n_chunks + jnp.maximum(c - wc + j, 0), colblk)
        kv_args = [z] * 6
        kv_specs = [pl.BlockSpec((CHUNK, ATTN_KV_W), kv_map(j, cb)) for cb in (kb, vb) for j in range(wc + 1)]
    else:
        def cache_map(j):
            return lambda b, c: (b * wc + j, 0)
        def new_map(colblk):
            return lambda b, c: (row_blk0 + b, colblk)
        kv_args = [cache_k, cache_k, z, cache_v, cache_v, z]
        kv_specs = []
        for cb in (kb, vb):
            kv_specs += [pl.BlockSpec((CHUNK, ATTN_KV_W), cache_map(j)) for j in range(wc)]
            kv_specs.append(pl.BlockSpec((CHUNK, ATTN_KV_W), new_map(cb)))
    return pl.pallas_call(
        functools.partial(_attn_body, prompt=cache_k is None),
        grid=(n_seq, n_chunks),
        in_specs=[pl.BlockSpec((CHUNK, ATTN_Q_W), q_map)] + kv_specs
                 + [pl.BlockSpec(bias.shape, lambda b, c: (0, 0))],
        out_specs=pl.BlockSpec((CHUNK, ATTN_Q_W), lambda b, c: (b * n_chunks + c, 0)),
        out_shape=jax.ShapeDtypeStruct((n_seq * n_chunks * CHUNK, ATTN_Q_W), BF16),
        compiler_params=_params("arbitrary", "arbitrary"),
        name="attn_prompt" if cache_k is None else "attn_sample",
    )(z, *kv_args, bias)


def _ret_body(*refs, has_init, n_chunks):
    sps = RET_SEQS_PER_STEP
    z_refs = refs[:4 * sps]
    decay_ref, dout_ref, kd_ref, g64_ref = refs[4 * sps:4 * sps + 4]
    rest = refs[4 * sps + 4:]
    if has_init:
        s0_ref, r_ref, sout_ref, state_ref = rest
    else:
        r_ref, sout_ref, state_ref = rest
    c = pl.program_id(2)
    hd = RET_HEAD_DIM

    @pl.when(c == 0)
    def _():
        if has_init:
            state_ref[...] = s0_ref[...]
        else:
            state_ref[...] = jnp.zeros(state_ref.shape, F32)

    for s in range(sps):
        q_ref, k_ref, v_ref, g_ref = z_refs[4 * s:4 * s + 4]
        for hh in range(RET_HEADS_PER_STEP):
            sl = slice(hh * hd, (hh + 1) * hd)
            q = q_ref[:, sl].astype(BF16)
            k = k_ref[:, sl] * (hd ** -0.5)
            v = v_ref[:, sl].astype(BF16)
            st = state_ref[s, hh]
            scores = lax.dot_general(q, k.astype(BF16), (((1,), (1,)), ((), ())),
                                     preferred_element_type=F32) * decay_ref[hh]
            o = jnp.dot(scores.astype(BF16), v, preferred_element_type=F32)
            o = o + jnp.dot(q, st.astype(BF16), preferred_element_type=F32) * dout_ref[hh]
            k_dec = (k * kd_ref[hh]).T.astype(BF16)
            state_ref[s, hh] = st * g64_ref[hh] + jnp.dot(k_dec, v, preferred_element_type=F32)
            rn = o * lax.rsqrt(jnp.mean(o * o, axis=-1, keepdims=True) + RMS_EPS)
            gate = g_ref[:, sl]
            r_ref[s, :, sl] = (rn * (gate * _sigmoid(gate))).astype(r_ref.dtype)

    @pl.when(c == n_chunks - 1)
    def _():
        sout_ref[...] = state_ref[...]


def _ret_consts():
    log_g = jnp.log1p(-(2.0 ** (-5.0 - jnp.arange(RET_HEADS, dtype=F32))))
    idx = jnp.arange(CHUNK, dtype=F32)
    rel = idx[:, None] - idx[None, :]
    decay = jnp.where(rel >= 0, jnp.exp(log_g[:, None, None] * jnp.maximum(rel, 0.0)), 0.0)
    dout = jnp.exp((idx[None, :] + 1.0) * log_g[:, None])
    kd = jnp.exp((CHUNK - 1.0 - idx)[None, :] * log_g[:, None])
    g64 = jnp.exp(CHUNK * log_g)
    bc = lambda a: jnp.broadcast_to(a[:, :, None], (RET_HEADS, a.shape[1], RET_HEAD_DIM))
    return decay, bc(dout), bc(kd), bc(g64[:, None])


def _retention(z, consts, row_blk0, n_seq, n_chunks, q_col, k_col, v_col, g_col, state0=None):
    hps, sps = RET_HEADS_PER_STEP, RET_SEQS_PER_STEP
    w = hps * RET_HEAD_DIM
    n_hp = RET_HEADS // hps
    decay, dout, kd, g64 = consts

    def zmap(s, col):
        cb = col // w
        return lambda b, h, c: (row_blk0 + (b * sps + s) * n_chunks + c, cb + h)

    cmap = lambda b, h, c: (h, 0, 0)
    in_specs = [pl.BlockSpec((CHUNK, w), zmap(s, col)) for s in range(sps) for col in (q_col, k_col, v_col, g_col)]
    in_specs += [pl.BlockSpec((hps, CHUNK, CHUNK), cmap),
                 pl.BlockSpec((hps, CHUNK, RET_HEAD_DIM), cmap),
                 pl.BlockSpec((hps, CHUNK, RET_HEAD_DIM), cmap),
                 pl.BlockSpec((hps, 1, RET_HEAD_DIM), cmap)]
    args = [z] * (4 * sps) + [decay, dout, kd, g64]
    st_spec = pl.BlockSpec((sps, hps, RET_HEAD_DIM, RET_HEAD_DIM), lambda b, h, c: (b, h, 0, 0))
    if state0 is not None:
        in_specs.append(st_spec)
        args.append(state0)
    r, state = pl.pallas_call(
        functools.partial(_ret_body, has_init=state0 is not None, n_chunks=n_chunks),
        grid=(n_seq // sps, n_hp, n_chunks),
        in_specs=in_specs,
        out_specs=[pl.BlockSpec((sps, CHUNK, w), lambda b, h, c: (b, c, h)), st_spec],
        out_shape=[jax.ShapeDtypeStruct((n_seq, n_chunks * CHUNK, RET_W), BF16),
                   jax.ShapeDtypeStruct((n_seq, RET_HEADS, RET_HEAD_DIM, RET_HEAD_DIM), F32)],
        scratch_shapes=[pltpu.VMEM((sps, hps, RET_HEAD_DIM, RET_HEAD_DIM), F32)],
        compiler_params=_params("arbitrary", "arbitrary", "arbitrary"),
        name="retention_prompt" if state0 is None else "retention_sample",
    )(*args)
    return r.reshape(n_seq * n_chunks * CHUNK, RET_W), state


def _router_body(x_ref, g_ref, whi_ref, wlo_ref, b_ref, tri_ref, hn_ref, e_ref, w_ref, rank_ref, cnt_ref,
                 carry_ref):
    ne, tb = b_ref.shape

    @pl.when(pl.program_id(0) == 0)
    def _():
        carry_ref[...] = jnp.zeros(carry_ref.shape, F32)

    hn = _rms(x_ref[...], g_ref[...])
    hn_ref[...] = hn
    hi = hn.astype(BF16)
    lo = (hn - hi.astype(F32)).astype(BF16)
    dn = (((1,), (1,)), ((), ()))
    logits = (lax.dot_general(whi_ref[...], hi, dn, preferred_element_type=F32)
              + lax.dot_general(whi_ref[...], lo, dn, preferred_element_type=F32)
              + lax.dot_general(wlo_ref[...], hi, dn, preferred_element_type=F32)) + b_ref[...]
    eidx = lax.broadcasted_iota(jnp.int32, (ne, tb), 0)
    rest = logits
    tops, idxs, hots = [], [], []
    for _ in range(TOP_K):
        m = jnp.max(rest, axis=0, keepdims=True)
        idx = jnp.min(jnp.where(rest == m, eidx, ne), axis=0, keepdims=True)
        hot = eidx == idx
        tops.append(m)
        idxs.append(idx)
        hots.append(hot)
        rest = jnp.where(hot, -jnp.inf, rest)
    ex = [jnp.exp(t - tops[0]) for t in tops]
    den = ex[0] + ex[1] + ex[2] + ex[3]
    member = jnp.zeros((ne, tb), F32)
    for hot in hots:
        member = member + jnp.where(hot, 1.0, 0.0)
    before = jnp.dot(member.astype(BF16), tri_ref[...], preferred_element_type=F32)
    carry = carry_ref[...]
    base = before + jnp.concatenate([carry] * (tb // carry.shape[1]), axis=1)
    ranks = [jnp.sum(jnp.where(hot, base, 0.0), axis=0, keepdims=True) for hot in hots]
    e_ref[...] = jnp.concatenate(idxs, axis=0)
    w_ref[...] = jnp.concatenate([e / den for e in ex], axis=0)
    rank_ref[...] = jnp.concatenate(ranks, axis=0).astype(jnp.int32)
    carry = carry + jnp.sum(member, axis=1, keepdims=True)
    carry_ref[...] = carry
    cnt_ref[...] = carry.astype(jnp.int32)


def _router(x1, g, w_router, b_router):
    t, d = x1.shape
    ne = w_router.shape[1]
    tb = ROUTER_TB
    wt = w_router.T
    wt_hi = wt.astype(BF16)
    wt_lo = (wt - wt_hi.astype(F32)).astype(BF16)
    bias = jnp.broadcast_to(b_router.astype(F32)[:, None], (ne, tb))
    tri = (jnp.arange(tb)[:, None] < jnp.arange(tb)[None, :]).astype(BF16)
    const = lambda shape: pl.BlockSpec(shape, lambda i: (0,) * len(shape))
    col = pl.BlockSpec((TOP_K, tb), lambda i: (0, i))
    return pl.pallas_call(
        _router_body,
        grid=(t // tb,),
        in_specs=[pl.BlockSpec((tb, d), lambda i: (i, 0)), const((1, d)), const((ne, d)), const((ne, d)),
                  const((ne, tb)), const((tb, tb))],
        out_specs=[pl.BlockSpec((tb, d), lambda i: (i, 0)), col, col, col, const((ne, 128))],
        out_shape=[jax.ShapeDtypeStruct((t, d), F32),
                   jax.ShapeDtypeStruct((TOP_K, t), jnp.int32),
                   jax.ShapeDtypeStruct((TOP_K, t), F32),
                   jax.ShapeDtypeStruct((TOP_K, t), jnp.int32),
                   jax.ShapeDtypeStruct((ne, 128), jnp.int32)],
        scratch_shapes=[pltpu.VMEM((ne, 128), F32)],
        compiler_params=_params("arbitrary"),
        name="router",
    )(x1, g.reshape(1, d), wt_hi, wt_lo, bias, tri)


DMA_ISSUE_UNROLL = 8
ROW_GROUP = 8
ROW_GROUP_UNROLL = 4


def _row_copy(src_hbm, dst, sem, src_row, dst_row):
    return pltpu.make_async_copy(src_hbm.at[pl.ds(src_row, 1), :], dst.at[pl.ds(dst_row, 1), :], sem)


def _rows_wait(src_hbm, dst, sem):
    pltpu.make_async_copy(src_hbm.at[pl.ds(0, dst.shape[0]), :], dst, sem).wait()


def _dispatch_body(nvb_ref, tok_ref, tok_next_ref, x_hbm, o_ref, buf_ref, sem):
    i = pl.program_id(0)
    nvb = nvb_ref[0]
    tm = buf_ref.shape[1]

    def issue(idx_ref, slot):
        def body(r, carry):
            _row_copy(x_hbm, buf_ref.at[slot], sem.at[slot], idx_ref[0, 0, r], r).start()
            return carry
        lax.fori_loop(0, tm, body, 0, unroll=DMA_ISSUE_UNROLL)

    @pl.when(i == 0)
    def _():
        issue(tok_ref, 0)

    @pl.when(i + 1 < nvb)
    def _():
        issue(tok_next_ref, (i + 1) % 2)

    @pl.when(i < nvb)
    def _():
        slot = i % 2
        _rows_wait(x_hbm, buf_ref.at[slot], sem.at[slot])

        o_ref[...] = buf_ref[slot].astype(o_ref.dtype)

    @pl.when(i >= nvb)
    def _():
        o_ref[...] = jnp.zeros(o_ref.shape, o_ref.dtype)


def _dispatch(hn, row_tok, nvb):
    t, d = hn.shape
    tm = DISPATCH_TM
    n_blocks = row_tok.shape[0] // tm
    tok_spec = lambda ahead: pl.BlockSpec(
        (1, 1, tm), lambda i, nvb_ref: (jnp.minimum(i + ahead, nvb_ref[0] - 1), 0, 0), memory_space=pltpu.SMEM)
    row_tok = row_tok.reshape(n_blocks, 1, tm)
    return pl.pallas_call(
        _dispatch_body,
        grid_spec=pltpu.PrefetchScalarGridSpec(
            num_scalar_prefetch=1,
            grid=(n_blocks,),
            in_specs=[tok_spec(0), tok_spec(1), pl.BlockSpec(memory_space=pl.ANY)],
            out_specs=pl.BlockSpec((tm, d), lambda i, nvb_ref: (i, 0)),
            scratch_shapes=[pltpu.VMEM((2, tm, d), F32), pltpu.SemaphoreType.DMA((2,))]),
        out_shape=jax.ShapeDtypeStruct((n_blocks * tm, d), BF16),
        compiler_params=_params("arbitrary"),
        name="moe_dispatch",
    )(nvb, row_tok, row_tok, hn)


SEG_UNITS = (8, 8, 4, 2, 1)


def _segments(nu):
    n8 = nu >> 3
    b4, b2, b1 = nu & 4, nu & 2, nu & 1
    off4 = n8 * 8
    active = (n8 >= 1, n8 >= 2, b4 != 0, b2 != 0, b1 != 0)
    first = (0, 8, off4, off4 + b4, off4 + b4 + b2)
    return active, first


def _unit_rows(base, first_unit, n_units):
    if isinstance(first_unit, int) and isinstance(base, int):
        return pl.ds(base + first_unit * MOE_TM, n_units * MOE_TM)
    return pl.ds(pl.multiple_of(base + first_unit * MOE_TM, MOE_TM), n_units * MOE_TM)


def _for_units(nu, fn):
    def body(u, carry):
        fn(u)
        return carry
    lax.fori_loop(0, nu, body, 0)


def _stream_segments(nu, compute, produce, out_copy, stage_refs, flag_ref, is_first_step, is_last_step):
    @pl.when(is_first_step)
    def _():
        for k in range(len(SEG_UNITS)):
            flag_ref[k] = 0

    active, first = _segments(nu)
    for k, (units, stage_ref) in enumerate(zip(SEG_UNITS, stage_refs)):
        @pl.when(active[k])
        def _(k=k, units=units, stage_ref=stage_ref):
            @pl.when(flag_ref[k] == 1)
            def _():
                out_copy(k, 0, units, None).wait()

            @pl.when(compute)
            def _():
                stage_ref[...] = produce(first[k], units).astype(stage_ref.dtype)

            @pl.when(jnp.logical_not(compute))
            def _():
                stage_ref[...] = jnp.zeros(stage_ref.shape, stage_ref.dtype)

            out_copy(k, first[k], units, True).start()
            flag_ref[k] = 1

    @pl.when(is_last_step)
    def _():
        for k, units in enumerate(SEG_UNITS):
            @pl.when(flag_ref[k] == 1)
            def _(k=k, units=units):
                out_copy(k, 0, units, None).wait()
                flag_ref[k] = 0


def _gateup_body(ie_ref, row0_ref, nu_ref, kind_ref, x_hbm, wg_ref, wl_ref, bg_ref, bl_ref, act_hbm,
                 xbuf_ref, wb_ref, st0, st1, st2, st3, st4, flag_ref, sem_x, sem_o, *, last):
    t, j = pl.program_id(0), pl.program_id(1)
    nu, r0 = nu_ref[t], row0_ref[t]
    compute = kind_ref[t] == 0
    tn = wg_ref.shape[1]
    stages = (st0, st1, st2, st3, st4)

    def x_copy(u):
        return pltpu.make_async_copy(x_hbm.at[_unit_rows(r0, u, 1), :], xbuf_ref.at[_unit_rows(0, u, 1), :], sem_x)

    def out_copy(k, first_unit, units, to_item):
        rows = _unit_rows(r0, first_unit, units) if to_item else pl.ds(0, units * MOE_TM)
        return pltpu.make_async_copy(stages[k], act_hbm.at[j, rows, :], sem_o.at[k])

    @pl.when(compute & (j == 0))
    def _():
        _for_units(nu, lambda u: x_copy(u).start())
        _for_units(nu, lambda u: x_copy(u).wait())

    @pl.when(compute & (nu > 0))
    def _():
        wb_ref[:, :tn] = wg_ref[...].astype(BF16)
        wb_ref[:, tn:] = wl_ref[...].astype(BF16)

    def produce(first_unit, units):
        gu = jnp.dot(xbuf_ref[_unit_rows(0, first_unit, units), :], wb_ref[...], preferred_element_type=F32)
        glu = jnp.minimum(gu[:, :tn] + bg_ref[...], SWIGLU_LIMIT)
        lin = jnp.clip(gu[:, tn:] + bl_ref[...], -SWIGLU_LIMIT, SWIGLU_LIMIT)
        return glu * _sigmoid(SWIGLU_ALPHA * glu) * (lin + 1.0)

    _stream_segments(nu, compute, produce, out_copy, stages, flag_ref,
                     (t == 0) & (j == 0),
                     (t == last[0]) & (j == last[1]))


def _down_body(ie_ref, row0_ref, nu_ref, kind_ref, a_hbm, w_ref, b_ref, y_hbm,
               abuf_ref, wb_ref, st0, st1, st2, st3, st4, flag_ref, sem_a, sem_o, *, last):
    t, j = pl.program_id(0), pl.program_id(1)
    nu, r0 = nu_ref[t], row0_ref[t]
    compute = kind_ref[t] == 0
    n_src, _, src_w = a_hbm.shape
    tn = w_ref.shape[1]
    stages = (st0, st1, st2, st3, st4)

    def a_copy(u, s):
        return pltpu.make_async_copy(a_hbm.at[s, _unit_rows(r0, u, 1), :],
                                     abuf_ref.at[_unit_rows(0, u, 1), pl.ds(s * src_w, src_w)], sem_a)

    def out_copy(k, first_unit, units, to_item):
        rows = _unit_rows(r0, first_unit, units) if to_item else pl.ds(0, units * MOE_TM)
        cols = pl.ds(pl.multiple_of(j * tn, tn), tn)
        return pltpu.make_async_copy(stages[k], y_hbm.at[rows, cols], sem_o.at[k])

    @pl.when(compute & (j == 0))
    def _():
        def start(u):
            for s in range(n_src):
                a_copy(u, s).start()

        def wait(u):
            for s in range(n_src):
                a_copy(u, s).wait()

        _for_units(nu, start)
        _for_units(nu, wait)

    @pl.when(compute & (nu > 0))
    def _():
        wb_ref[...] = w_ref[...].astype(BF16)

    def produce(first_unit, units):
        return jnp.dot(abuf_ref[_unit_rows(0, first_unit, units), :], wb_ref[...],
                       preferred_element_type=F32) + b_ref[...]

    _stream_segments(nu, compute, produce, out_copy, stages, flag_ref,
                     (t == 0) & (j == 0),
                     (t == last[0]) & (j == last[1]))


def _item_specs(n_tiles):
    def tile(j, t, nu, kind):
        return jnp.where((kind[t] == 0) & (nu[t] > 0), j, n_tiles - 1)
    return lambda off: (lambda t, j, ie, row0, nu, kind: (ie[t], 0, off + tile(j, t, nu, kind)))


def _moe_scratch(k_dim, wb_cols, tn, stage_dtype):
    return ([pltpu.VMEM((MOE_ITEM_UNITS * MOE_TM, k_dim), BF16), pltpu.VMEM((k_dim, wb_cols), BF16)]
            + [pltpu.VMEM((units * MOE_TM, tn), stage_dtype) for units in SEG_UNITS]
            + [pltpu.SMEM((len(SEG_UNITS),), jnp.int32),
               pltpu.SemaphoreType.DMA(()),
               pltpu.SemaphoreType.DMA((len(SEG_UNITS),))])


def _gateup(xs, w_gate_up, b_gate_up, items):
    rows, d = xs.shape
    ne, _, two_f = w_gate_up.shape
    f = two_f // 2
    tn = MOE_UP_TN
    nt = f // tn
    wmap = _item_specs(nt)
    n_items = items[0].shape[0]
    return pl.pallas_call(
        functools.partial(_gateup_body, last=(n_items - 1, nt - 1)),
        grid_spec=pltpu.PrefetchScalarGridSpec(
            num_scalar_prefetch=4,
            grid=(n_items, nt),
            in_specs=[pl.BlockSpec(memory_space=pl.ANY),
                      pl.BlockSpec((None, d, tn), wmap(0)),
                      pl.BlockSpec((None, d, tn), wmap(nt)),
                      pl.BlockSpec((None, 1, tn), wmap(0)),
                      pl.BlockSpec((None, 1, tn), wmap(nt))],
            out_specs=pl.BlockSpec(memory_space=pl.ANY),
            scratch_shapes=_moe_scratch(d, 2 * tn, tn, BF16)),
        out_shape=jax.ShapeDtypeStruct((nt, rows, tn), BF16),
        compiler_params=_params("arbitrary", "arbitrary"),
        name="moe_gate_up",
    )(*items, xs, w_gate_up, w_gate_up,
      b_gate_up.reshape(ne, 1, two_f), b_gate_up.reshape(ne, 1, two_f))


def _down(act, w_down, b_down, items):
    n_src, rows, src_w = act.shape
    ne, f, d = w_down.shape
    tn = MOE_DOWN_TN
    nt = d // tn
    wmap = _item_specs(nt)
    n_items = items[0].shape[0]
    return pl.pallas_call(
        functools.partial(_down_body, last=(n_items - 1, nt - 1)),
        grid_spec=pltpu.PrefetchScalarGridSpec(
            num_scalar_prefetch=4,
            grid=(n_items, nt),
            in_specs=[pl.BlockSpec(memory_space=pl.ANY),
                      pl.BlockSpec((None, f, tn), wmap(0)),
                      pl.BlockSpec((None, 1, tn), wmap(0))],
            out_specs=pl.BlockSpec(memory_space=pl.ANY),
            scratch_shapes=_moe_scratch(f, tn, tn, F32)),
        out_shape=jax.ShapeDtypeStruct((rows, d), F32),
        compiler_params=_params("arbitrary", "arbitrary"),
        name="moe_down",
    )(*items, act, w_down, b_down.reshape(ne, 1, d))


def _combine_body(dest_ref, dest_next_ref, y_hbm, x_ref, w_ref, g_ref, o_ref, buf_ref, sem, *, n_steps):
    i = pl.program_id(0)
    tb = x_ref.shape[0]

    def issue(idx_ref, slot):
        for k in range(TOP_K):
            def body(t, carry, k=k):
                _row_copy(y_hbm, buf_ref.at[slot, k], sem.at[slot], idx_ref[0, k, t], t).start()
                return carry
            lax.fori_loop(0, tb, body, 0, unroll=DMA_ISSUE_UNROLL)

    @pl.when(i == 0)
    def _():
        issue(dest_ref, 0)

    @pl.when(i + 1 < n_steps)
    def _():
        issue(dest_next_ref, (i + 1) % 2)

    slot = i % 2
    for k in range(TOP_K):
        _rows_wait(y_hbm, buf_ref.at[slot, k], sem.at[slot])

    def group(c, carry):
        sl = pl.ds(pl.multiple_of(c * ROW_GROUP, ROW_GROUP), ROW_GROUP)
        w = w_ref[sl, :]
        y = buf_ref[slot, 0, sl, :] * w[:, 0:1]
        for k in range(1, TOP_K):
            y = y + buf_ref[slot, k, sl, :] * w[:, k:k + 1]
        o_ref[sl, :] = _rms(x_ref[sl, :] + y, g_ref[...])
        return carry
    lax.fori_loop(0, tb // ROW_GROUP, group, 0, unroll=ROW_GROUP_UNROLL)


def _combine(y_rows, x1, dest, top_w, g, tok0, n_tok):
    d = x1.shape[1]
    tb = COMBINE_TB
    blk0, nb = tok0 // tb, n_tok // tb
    dest_b = dest.reshape(TOP_K, -1, tb).transpose(1, 0, 2)
    dest_spec = lambda ahead: pl.BlockSpec(
        (1, TOP_K, tb), lambda i: (blk0 + jnp.minimum(i + ahead, nb - 1), 0, 0), memory_space=pltpu.SMEM)
    return pl.pallas_call(
        functools.partial(_combine_body, n_steps=nb),
        grid=(nb,),
        in_specs=[dest_spec(0), dest_spec(1),
                  pl.BlockSpec(memory_space=pl.ANY),
                  pl.BlockSpec((tb, d), lambda i: (blk0 + i, 0)),
                  pl.BlockSpec((tb, TOP_K), lambda i: (blk0 + i, 0)),
                  pl.BlockSpec((1, d), lambda i: (0, 0))],
        out_specs=pl.BlockSpec((tb, d), lambda i: (i, 0)),
        out_shape=jax.ShapeDtypeStruct((n_tok, d), F32),
        scratch_shapes=[pltpu.VMEM((2, TOP_K, tb, d), F32), pltpu.SemaphoreType.DMA((2,))],
        compiler_params=_params("arbitrary"),
        name="moe_combine_prompt" if tok0 == 0 else "moe_combine_sample",
    )(dest_b, dest_b, y_rows, x1, top_w.T, g.reshape(1, d))


def _pick(table, idx):
    hot = idx[..., None] == jnp.arange(table.shape[0], dtype=idx.dtype)
    return jnp.sum(jnp.where(hot, table, 0), axis=-1)


def _plan(top_e, rank, counts, n_units, n_items):
    u, cap = MOE_TM, MOE_ITEM_UNITS
    n_tok = top_e.shape[1]
    i32 = jnp.int32
    units = (counts + u - 1) // u
    uend = jnp.cumsum(units)
    ustart = uend - units
    nvu = uend[-1]
    dest = _pick(ustart * u, top_e) + rank
    tok = jnp.broadcast_to(jnp.arange(n_tok, dtype=i32)[None, :], dest.shape)
    row_tok = jnp.zeros((n_units * u,), i32).at[dest.reshape(-1)].set(tok.reshape(-1))
    nvb = (nvu * u + DISPATCH_TM - 1) // DISPATCH_TM

    per_e = (units + cap - 1) // cap
    iend = jnp.cumsum(per_e)
    n_compute = iend[-1]
    t = jnp.arange(n_items, dtype=i32)
    is_c = t < n_compute
    e_t = jnp.minimum(jnp.sum(jnp.minimum(t, n_compute - 1)[:, None] >= iend[None, :], axis=1), N_EXPERTS - 1).astype(i32)
    k_t = t - _pick(iend - per_e, e_t)
    z_t = t - n_compute
    row0 = jnp.where(is_c, (_pick(ustart, e_t) + k_t * cap) * u, (nvu + z_t * cap) * u)
    nu = jnp.where(is_c, _pick(units, e_t) - k_t * cap, (n_units - nvu) - z_t * cap)
    nu = jnp.clip(nu, 0, cap)
    row0 = jnp.where(nu > 0, row0, 0)
    kind = jnp.where(is_c, 0, 1)
    items = tuple(a.astype(i32) for a in (e_t, row0, nu, kind))
    return dest, row_tok, nvb.astype(i32).reshape(1), items


def kernel(x_prompt, x_sample, cache_attn_k, cache_attn_v, state_ret, norm_mix, w_in, attn_sinks,
           w_o_attn, w_o_ret, w_out, norm_ffn, w_router, b_router, w_gate_up, b_gate_up,
           w_down, b_down, norm_final):
    bp, sp, d = x_prompt.shape
    bs, ss, _ = x_sample.shape
    depth = w_in.shape[0]
    assert depth == 1 and ss == CHUNK and sp % CHUNK == 0
    tp, ts = bp * sp, bs * ss
    t = tp + ts
    ncp = sp // CHUNK

    widths = (ATTN_Q_W, ATTN_KV_W, ATTN_KV_W, RET_W, RET_W, RET_W, RET_W, d, d)
    cols = [0]
    for wdt in widths:
        cols.append(cols[-1] + wdt)
    c_aq, c_ak, c_av, c_rq, c_rk, c_rv, c_rg, c_ga, c_gr = cols[:9]

    x = (x_prompt.reshape(tp, d), x_sample.reshape(ts, d))
    h = _rmsnorm(x[0], x[1], norm_mix[0], BF16)
    z = _matmul(h, w_in[0], F32, "in_proj")

    bias = _attn_bias(attn_sinks[0])
    n_win = cache_attn_k.shape[2]
    ck = cache_attn_k[0].reshape(bs * n_win, ATTN_KV_W)
    cv = cache_attn_v[0].reshape(bs * n_win, ATTN_KV_W)
    attn_p = _attention(z, bias, 0, bp, ncp, c_ak, c_av)
    attn_s = _attention(z, bias, tp // CHUNK, bs, 1, c_ak, c_av, ck, cv)

    consts = _ret_consts()
    r_p, state_p = _retention(z, consts, 0, bp, ncp, c_rq, c_rk, c_rv, c_rg)
    r_s, state_s = _retention(z, consts, tp // CHUNK, bs, 1, c_rq, c_rk, c_rv, c_rg, state_ret[0])

    merged = _merge((attn_p, attn_s), (r_p, r_s), w_o_attn[0], w_o_ret[0], z, c_ga, c_gr)
    x1 = _matmul(merged, w_out[0], F32, "out_proj", residual=x)

    hn, top_e, top_w, rank, cnt = _router(x1, norm_ffn[0], w_router[0], b_router[0])
    n_units = -(-(t * TOP_K) // MOE_TM) + N_EXPERTS
    n_units += -n_units % (DISPATCH_TM // MOE_TM)
    n_items = N_EXPERTS + n_units // MOE_ITEM_UNITS + 2
    dest, row_tok, nvb, items = _plan(top_e, rank, cnt[:, 0], n_units, n_items)
    xs = _dispatch(hn, row_tok, nvb)
    act = _gateup(xs, w_gate_up[0], b_gate_up[0], items)
    y_rows = _down(act, w_down[0], b_down[0], items)
    y_prompt = _combine(y_rows, x1, dest, top_w, norm_final, 0, tp).reshape(bp, sp, d)
    y_sample = _combine(y_rows, x1, dest, top_w, norm_final, tp, ts).reshape(bs, ss, d)

    n_keep = min(WINDOW, sp)
    kv = z[:, c_ak:c_ak + 2 * ATTN_KV_W]
    kv_p = kv[:tp].reshape(bp, sp, 2, ATTN_KV_HEADS, ATTN_HEAD_DIM)[:, sp - n_keep:]
    kv_s = kv[tp:].reshape(bs, ss, 2, ATTN_KV_HEADS, ATTN_HEAD_DIM)
    return (y_prompt, y_sample, kv_p[:, :, 0][None], kv_p[:, :, 1][None], state_p[None],
            kv_s[:, :, 0][None], kv_s[:, :, 1][None], state_s[None])
```

```python
import functools

import jax
import jax.numpy as jnp
from jax import lax
from jax.experimental import pallas as pl
from jax.experimental.pallas import tpu as pltpu

F32 = jnp.float32
BF16 = jnp.bfloat16

CHUNK = 64
WINDOW = 128
ATTN_HEADS = 32
ATTN_KV_HEADS = 4
ATTN_GROUP = ATTN_HEADS // ATTN_KV_HEADS
ATTN_HEAD_DIM = 64
ATTN_Q_W = ATTN_HEADS * ATTN_HEAD_DIM
ATTN_KV_W = ATTN_KV_HEADS * ATTN_HEAD_DIM
RET_HEADS = 8
RET_HEAD_DIM = 256
RET_W = RET_HEADS * RET_HEAD_DIM
N_EXPERTS = 32
TOP_K = 4
SWIGLU_ALPHA = 1.702
SWIGLU_LIMIT = 7.0
RMS_EPS = 1e-5
NEG_INF = -1e30

ATTN_COLS = 256
RET_HEADS_PER_STEP = 2
RET_SEQS_PER_STEP = 4

VMEM_LIMIT_BYTES = 56 * 1024 * 1024

MOE_TM = 128
MOE_ITEM_UNITS = 16
DISPATCH_TM = 256
MOE_UP_TN = 256
MOE_DOWN_TN = 512
PROJ_TM = 1024
PROJ_TN = 512
MERGE_TM = 512
COMBINE_TB = 128
ROUTER_TB = 256
NORM_TM = 256


def _params(*semantics):
    return pltpu.CompilerParams(dimension_semantics=semantics, vmem_limit_bytes=VMEM_LIMIT_BYTES)


def _rms(x, g):
    ms = jnp.mean(x * x, axis=-1, keepdims=True)
    return x * lax.rsqrt(ms + RMS_EPS) * g


def _sigmoid(x):
    return 1.0 / (1.0 + jnp.exp(-x))


def _first(i, nf):
    return jnp.minimum(i, nf - 1)


def _second(i, nf):
    return jnp.maximum(i - nf, 0)


def _on_rows(i, nf, fn, first_refs, second_refs):
    @pl.when(i < nf)
    def _():
        fn(*first_refs)

    @pl.when(i >= nf)
    def _():
        fn(*second_refs)


def _rmsnorm_body(xp_ref, xs_ref, g_ref, o_ref, *, nf):
    def norm(x_ref):
        o_ref[...] = _rms(x_ref[...], g_ref[...]).astype(o_ref.dtype)

    _on_rows(pl.program_id(0), nf, norm, (xp_ref,), (xs_ref,))


def _rmsnorm(xp, xs, g, out_dtype):
    (mp, d), ms = xp.shape, xs.shape[0]
    tm = min(NORM_TM, ms)
    nf = mp // tm
    return pl.pallas_call(
        functools.partial(_rmsnorm_body, nf=nf),
        grid=((mp + ms) // tm,),
        in_specs=[pl.BlockSpec((tm, d), lambda i: (_first(i, nf), 0)),
                  pl.BlockSpec((tm, d), lambda i: (_second(i, nf), 0)),
                  pl.BlockSpec((1, d), lambda i: (0, 0))],
        out_specs=pl.BlockSpec((tm, d), lambda i: (i, 0)),
        out_shape=jax.ShapeDtypeStruct((mp + ms, d), out_dtype),
        compiler_params=_params("arbitrary"),
        name="rmsnorm",
    )(xp, xs, g.reshape(1, d))


def _matmul_body(x_ref, w_ref, *rest, nf):
    if nf is None:
        o_ref, wb_ref = rest
    else:
        rp_ref, rs_ref, o_ref, wb_ref = rest
    i = pl.program_id(1)

    @pl.when(i == 0)
    def _():
        wb_ref[...] = w_ref[...].astype(BF16)

    acc = jnp.dot(x_ref[...], wb_ref[...], preferred_element_type=F32)
    if nf is None:
        o_ref[...] = acc.astype(o_ref.dtype)
    else:
        def add(r_ref):
            o_ref[...] = (acc + r_ref[...]).astype(o_ref.dtype)

        _on_rows(i, nf, add, (rp_ref,), (rs_ref,))


def _matmul(x, w, out_dtype, name, residual=None):
    m, k = x.shape
    n = w.shape[1]
    tn = min(PROJ_TN, n)
    in_specs = [None, pl.BlockSpec((k, tn), lambda j, i: (0, j))]
    args = [x, w]
    nf = None
    if residual is None:
        tm = min(PROJ_TM, m)
    else:
        tm = min(PROJ_TM, residual[1].shape[0])
        nf = residual[0].shape[0] // tm
        in_specs += [pl.BlockSpec((tm, tn), lambda j, i: (_first(i, nf), j)),
                     pl.BlockSpec((tm, tn), lambda j, i: (_second(i, nf), j))]
        args += list(residual)
    in_specs[0] = pl.BlockSpec((tm, k), lambda j, i: (i, 0))
    return pl.pallas_call(
        functools.partial(_matmul_body, nf=nf),
        grid=(n // tn, m // tm),
        in_specs=in_specs,
        out_specs=pl.BlockSpec((tm, tn), lambda j, i: (i, j)),
        out_shape=jax.ShapeDtypeStruct((m, n), out_dtype),
        scratch_shapes=[pltpu.VMEM((k, tn), BF16)],
        compiler_params=_params("arbitrary", "arbitrary"),
        name=name,
    )(*args)


def _merge_body(ap_ref, as_ref, rp_ref, rs_ref, wa_ref, wr_ref, ga_ref, gr_ref, o_ref, wab_ref, wrb_ref, *, nf):
    i = pl.program_id(1)

    @pl.when(i == 0)
    def _():
        wab_ref[...] = wa_ref[...].astype(BF16)
        wrb_ref[...] = wr_ref[...].astype(BF16)

    def merge(a_ref, r_ref):
        ya = jnp.dot(a_ref[...], wab_ref[...], preferred_element_type=F32)
        yr = jnp.dot(r_ref[...], wrb_ref[...], preferred_element_type=F32)
        o_ref[...] = (_sigmoid(ga_ref[...]) * ya + _sigmoid(gr_ref[...]) * yr).astype(o_ref.dtype)

    _on_rows(i, nf, merge, (ap_ref, rp_ref), (as_ref, rs_ref))


def _merge(a, r, w_a, w_r, z, ga_col, gr_col):
    (mp, ka), ms = a[0].shape, a[1].shape[0]
    kr = r[0].shape[1]
    n = w_a.shape[1]
    tm, tn = min(MERGE_TM, ms), min(PROJ_TN, n)
    nf = mp // tm
    ga_blk, gr_blk = ga_col // tn, gr_col // tn
    rows = lambda k, pick: pl.BlockSpec((tm, k), lambda j, i: (pick(i, nf), 0))
    return pl.pallas_call(
        functools.partial(_merge_body, nf=nf),
        grid=(n // tn, (mp + ms) // tm),
        in_specs=[rows(ka, _first), rows(ka, _second), rows(kr, _first), rows(kr, _second),
                  pl.BlockSpec((ka, tn), lambda j, i: (0, j)),
                  pl.BlockSpec((kr, tn), lambda j, i: (0, j)),
                  pl.BlockSpec((tm, tn), lambda j, i: (i, ga_blk + j)),
                  pl.BlockSpec((tm, tn), lambda j, i: (i, gr_blk + j))],
        out_specs=pl.BlockSpec((tm, tn), lambda j, i: (i, j)),
        out_shape=jax.ShapeDtypeStruct((mp + ms, n), BF16),
        scratch_shapes=[pltpu.VMEM((ka, tn), BF16), pltpu.VMEM((kr, tn), BF16)],
        compiler_params=_params("arbitrary", "arbitrary"),
        name="gated_merge",
    )(a[0], a[1], r[0], r[1], w_a, w_r, z, z)


def _attn_body(q_ref, k0_ref, k1_ref, k2_ref, v0_ref, v1_ref, v2_ref, bias_ref, o_ref, *, prompt):
    hd, grp = ATTN_HEAD_DIM, ATTN_GROUP
    rows = grp * CHUNK
    n_keys = WINDOW + CHUNK
    pad = jnp.zeros((ATTN_COLS - n_keys, hd), F32)
    if prompt:
        n_invalid = jnp.maximum(WINDOW - pl.program_id(1) * CHUNK, 0)
        col = lax.broadcasted_iota(jnp.int32, (rows, ATTN_COLS), 1)
    for h in range(ATTN_KV_HEADS):
        sl = slice(h * hd, (h + 1) * hd)
        kh = jnp.concatenate([k0_ref[:, sl], k1_ref[:, sl], k2_ref[:, sl], pad], axis=0).astype(BF16)
        vh = jnp.concatenate([v0_ref[:, sl], v1_ref[:, sl], v2_ref[:, sl], pad], axis=0).astype(BF16)
        qh = jnp.concatenate(
            [q_ref[:, (h * grp + g) * hd:(h * grp + g + 1) * hd] for g in range(grp)], axis=0).astype(BF16)
        s = lax.dot_general(qh, kh, (((1,), (1,)), ((), ())), preferred_element_type=F32)
        s = s * (hd ** -0.5) + bias_ref[h * rows:(h + 1) * rows, :]
        if prompt:
            s = jnp.where(col < n_invalid, NEG_INF, s)
        m = jnp.max(s, axis=-1, keepdims=True)
        p = jnp.exp(s - m)
        p = (p / jnp.sum(p, axis=-1, keepdims=True)).astype(BF16)
        oh = jnp.dot(p, vh, preferred_element_type=F32)
        for g in range(grp):
            off = (h * grp + g) * hd
            o_ref[:, off:off + hd] = oh[g * CHUNK:(g + 1) * CHUNK, :].astype(o_ref.dtype)


def _attn_bias(sinks):
    n_keys = WINDOW + CHUNK
    slopes = 2.0 ** (-8.0 * jnp.arange(1, ATTN_HEADS + 1, dtype=F32) / ATTN_HEADS)
    q_pos = WINDOW + jnp.arange(CHUNK)
    k_pos = jnp.arange(n_keys)
    dist = jnp.abs(q_pos[:, None] - k_pos[None, :]).astype(F32)
    alibi = -(slopes[:, None, None] * dist[None])
    sink = jnp.broadcast_to(sinks.astype(F32)[:, None, None], (ATTN_HEADS, CHUNK, 1))
    fill = jnp.full((ATTN_HEADS, CHUNK, ATTN_COLS - n_keys - 1), NEG_INF, F32)
    return jnp.concatenate([alibi, sink, fill], axis=-1).reshape(ATTN_HEADS * CHUNK, ATTN_COLS)


def _attention(z, bias, row_blk0, n_seq, n_chunks, k_col, v_col, cache_k=None, cache_v=None):
    kb, vb = k_col // ATTN_KV_W, v_col // ATTN_KV_W
    wc = WINDOW // CHUNK

    def q_map(b, c):
        return (row_blk0 + b * n_chunks + c, 0)

    if cache_k is None:
        def kv_map(j, colblk):
            return lambda b, c: (row_blk0 + b * n_chunks + jnp.maximum(c - wc + j, 0), colblk)
        kv_args = [z] * 6
        kv_specs = [pl.BlockSpec((CHUNK, ATTN_KV_W), kv_map(j, cb)) for cb in (kb, vb) for j in range(wc + 1)]
    else:
        def cache_map(j):
            return lambda b, c: (b * wc + j, 0)
        def new_map(colblk):
            return lambda b, c: (row_blk0 + b, colblk)
        kv_args = [cache_k, cache_k, z, cache_v, cache_v, z]
        kv_specs = []
        for cb in (kb, vb):
            kv_specs += [pl.BlockSpec((CHUNK, ATTN_KV_W), cache_map(j)) for j in range(wc)]
            kv_specs.append(pl.BlockSpec((CHUNK, ATTN_KV_W), new_map(cb)))
    return pl.pallas_call(
        functools.partial(_attn_body, prompt=cache_k is None),
        grid=(n_seq, n_chunks),
        in_specs=[pl.BlockSpec((CHUNK, ATTN_Q_W), q_map)] + kv_specs
                 + [pl.BlockSpec(bias.shape, lambda b, c: (0, 0))],
        out_specs=pl.BlockSpec((CHUNK, ATTN_Q_W), lambda b, c: (b * n_chunks + c, 0)),
        out_shape=jax.ShapeDtypeStruct((n_seq * n_chunks * CHUNK, ATTN_Q_W), BF16),
        compiler_params=_params("arbitrary", "arbitrary"),
        name="attn_prompt" if cache_k is None else "attn_sample",
    )(z, *kv_args, bias)


def _ret_body(*refs, has_init, n_chunks):
    sps = RET_SEQS_PER_STEP
    z_refs = refs[:4 * sps]
    decay_ref, dout_ref, kd_ref, g64_ref = refs[4 * sps:4 * sps + 4]
    rest = refs[4 * sps + 4:]
    if has_init:
        s0_ref, r_ref, sout_ref, state_ref = rest
    else:
        r_ref, sout_ref, state_ref = rest
    c = pl.program_id(2)
    hd = RET_HEAD_DIM

    @pl.when(c == 0)
    def _():
        if has_init:
            state_ref[...] = s0_ref[...]
        else:
            state_ref[...] = jnp.zeros(state_ref.shape, F32)

    for s in range(sps):
        q_ref, k_ref, v_ref, g_ref = z_refs[4 * s:4 * s + 4]
        for hh in range(RET_HEADS_PER_STEP):
            sl = slice(hh * hd, (hh + 1) * hd)
            q = q_ref[:, sl].astype(BF16)
            k = k_ref[:, sl] * (hd ** -0.5)
            v = v_ref[:, sl].astype(BF16)
            st = state_ref[s, hh]
            scores = lax.dot_general(q, k.astype(BF16), (((1,), (1,)), ((), ())),
                                     preferred_element_type=F32) * decay_ref[hh]
            o = jnp.dot(scores.astype(BF16), v, preferred_element_type=F32)
            o = o + jnp.dot(q, st.astype(BF16), preferred_element_type=F32) * dout_ref[hh]
            k_dec = (k * kd_ref[hh]).T.astype(BF16)
            state_ref[s, hh] = st * g64_ref[hh] + jnp.dot(k_dec, v, preferred_element_type=F32)
            rn = o * lax.rsqrt(jnp.mean(o * o, axis=-1, keepdims=True) + RMS_EPS)
            gate = g_ref[:, sl]
            r_ref[s, :, sl] = (rn * (gate * _sigmoid(gate))).astype(r_ref.dtype)

    @pl.when(c == n_chunks - 1)
    def _():
        sout_ref[...] = state_ref[...]


def _ret_consts():
    log_g = jnp.log1p(-(2.0 ** (-5.0 - jnp.arange(RET_HEADS, dtype=F32))))
    idx = jnp.arange(CHUNK, dtype=F32)
    rel = idx[:, None] - idx[None, :]
    decay = jnp.where(rel >= 0, jnp.exp(log_g[:, None, None] * jnp.maximum(rel, 0.0)), 0.0)
    dout = jnp.exp((idx[None, :] + 1.0) * log_g[:, None])
    kd = jnp.exp((CHUNK - 1.0 - idx)[None, :] * log_g[:, None])
    g64 = jnp.exp(CHUNK * log_g)
    bc = lambda a: jnp.broadcast_to(a[:, :, None], (RET_HEADS, a.shape[1], RET_HEAD_DIM))
    return decay, bc(dout), bc(kd), bc(g64[:, None])


def _retention(z, consts, row_blk0, n_seq, n_chunks, q_col, k_col, v_col, g_col, state0=None):
    hps, sps = RET_HEADS_PER_STEP, RET_SEQS_PER_STEP
    w = hps * RET_HEAD_DIM
    n_hp = RET_HEADS // hps
    decay, dout, kd, g64 = consts

    def zmap(s, col):
        cb = col // w
        return lambda b, h, c: (row_blk0 + (b * sps + s) * n_chunks + c, cb + h)

    cmap = lambda b, h, c: (h, 0, 0)
    in_specs = [pl.BlockSpec((CHUNK, w), zmap(s, col)) for s in range(sps) for col in (q_col, k_col, v_col, g_col)]
    in_specs += [pl.BlockSpec((hps, CHUNK, CHUNK), cmap),
                 pl.BlockSpec((hps, CHUNK, RET_HEAD_DIM), cmap),
                 pl.BlockSpec((hps, CHUNK, RET_HEAD_DIM), cmap),
                 pl.BlockSpec((hps, 1, RET_HEAD_DIM), cmap)]
    args = [z] * (4 * sps) + [decay, dout, kd, g64]
    st_spec = pl.BlockSpec((sps, hps, RET_HEAD_DIM, RET_HEAD_DIM), lambda b, h, c: (b, h, 0, 0))
    if state0 is not None:
        in_specs.append(st_spec)
        args.append(state0)
    r, state = pl.pallas_call(
        functools.partial(_ret_body, has_init=state0 is not None, n_chunks=n_chunks),
        grid=(n_seq // sps, n_hp, n_chunks),
        in_specs=in_specs,
        out_specs=[pl.BlockSpec((sps, CHUNK, w), lambda b, h, c: (b, c, h)), st_spec],
        out_shape=[jax.ShapeDtypeStruct((n_seq, n_chunks * CHUNK, RET_W), BF16),
                   jax.ShapeDtypeStruct((n_seq, RET_HEADS, RET_HEAD_DIM, RET_HEAD_DIM), F32)],
        scratch_shapes=[pltpu.VMEM((sps, hps, RET_HEAD_DIM, RET_HEAD_DIM), F32)],
        compiler_params=_params("arbitrary", "arbitrary", "arbitrary"),
        name="retention_prompt" if state0 is None else "retention_sample",
    )(*args)
    return r.reshape(n_seq * n_chunks * CHUNK, RET_W), state


def _router_body(x_ref, g_ref, whi_ref, wlo_ref, b_ref, tri_ref, hn_ref, e_ref, w_ref, rank_ref, cnt_ref,
                 carry_ref):
    ne, tb = b_ref.shape

    @pl.when(pl.program_id(0) == 0)
    def _():
        carry_ref[...] = jnp.zeros(carry_ref.shape, F32)

    hn = _rms(x_ref[...], g_ref[...])
    hn_ref[...] = hn
    hi = hn.astype(BF16)
    lo = (hn - hi.astype(F32)).astype(BF16)
    dn = (((1,), (1,)), ((), ()))
    logits = (lax.dot_general(whi_ref[...], hi, dn, preferred_element_type=F32)
              + lax.dot_general(whi_ref[...], lo, dn, preferred_element_type=F32)
              + lax.dot_general(wlo_ref[...], hi, dn, preferred_element_type=F32)) + b_ref[...]
    eidx = lax.broadcasted_iota(jnp.int32, (ne, tb), 0)
    rest = logits
    tops, idxs, hots = [], [], []
    for _ in range(TOP_K):
        m = jnp.max(rest, axis=0, keepdims=True)
        idx = jnp.min(jnp.where(rest == m, eidx, ne), axis=0, keepdims=True)
        hot = eidx == idx
        tops.append(m)
        idxs.append(idx)
        hots.append(hot)
        rest = jnp.where(hot, -jnp.inf, rest)
    ex = [jnp.exp(t - tops[0]) for t in tops]
    den = ex[0] + ex[1] + ex[2] + ex[3]
    member = jnp.zeros((ne, tb), F32)
    for hot in hots:
        member = member + jnp.where(hot, 1.0, 0.0)
    before = jnp.dot(member.astype(BF16), tri_ref[...], preferred_element_type=F32)
    carry = carry_ref[...]
    base = before + jnp.concatenate([carry] * (tb // carry.shape[1]), axis=1)
    ranks = [jnp.sum(jnp.where(hot, base, 0.0), axis=0, keepdims=True) for hot in hots]
    e_ref[...] = jnp.concatenate(idxs, axis=0)
    w_ref[...] = jnp.concatenate([e / den for e in ex], axis=0)
    rank_ref[...] = jnp.concatenate(ranks, axis=0).astype(jnp.int32)
    carry = carry + jnp.sum(member, axis=1, keepdims=True)
    carry_ref[...] = carry
    cnt_ref[...] = carry.astype(jnp.int32)


def _router(x1, g, w_router, b_router):
    t, d = x1.shape
    ne = w_router.shape[1]
    tb = ROUTER_TB
    wt = w_router.T
    wt_hi = wt.astype(BF16)
    wt_lo = (wt - wt_hi.astype(F32)).astype(BF16)
    bias = jnp.broadcast_to(b_router.astype(F32)[:, None], (ne, tb))
    tri = (jnp.arange(tb)[:, None] < jnp.arange(tb)[None, :]).astype(BF16)
    const = lambda shape: pl.BlockSpec(shape, lambda i: (0,) * len(shape))
    col = pl.BlockSpec((TOP_K, tb), lambda i: (0, i))
    return pl.pallas_call(
        _router_body,
        grid=(t // tb,),
        in_specs=[pl.BlockSpec((tb, d), lambda i: (i, 0)), const((1, d)), const((ne, d)), const((ne, d)),
                  const((ne, tb)), const((tb, tb))],
        out_specs=[pl.BlockSpec((tb, d), lambda i: (i, 0)), col, col, col, const((ne, 128))],
        out_shape=[jax.ShapeDtypeStruct((t, d), F32),
                   jax.ShapeDtypeStruct((TOP_K, t), jnp.int32),
                   jax.ShapeDtypeStruct((TOP_K, t), F32),
                   jax.ShapeDtypeStruct((TOP_K, t), jnp.int32),
                   jax.ShapeDtypeStruct((ne, 128), jnp.int32)],
        scratch_shapes=[pltpu.VMEM((ne, 128), F32)],
        compiler_params=_params("arbitrary"),
        name="router",
    )(x1, g.reshape(1, d), wt_hi, wt_lo, bias, tri)


DMA_ISSUE_UNROLL = 8
ROW_GROUP = 8
ROW_GROUP_UNROLL = 4


def _row_copy(src_hbm, dst, sem, src_row, dst_row):
    return pltpu.make_async_copy(src_hbm.at[pl.ds(src_row, 1), :], dst.at[pl.ds(dst_row, 1), :], sem)


def _rows_wait(src_hbm, dst, sem):
    pltpu.make_async_copy(src_hbm.at[pl.ds(0, dst.shape[0]), :], dst, sem).wait()


def _dispatch_body(nvb_ref, tok_ref, tok_next_ref, x_hbm, o_ref, buf_ref, sem):
    i = pl.program_id(0)
    nvb = nvb_ref[0]
    tm = buf_ref.shape[1]

    def issue(idx_ref, slot):
        def body(r, carry):
            _row_copy(x_hbm, buf_ref.at[slot], sem.at[slot], idx_ref[0, 0, r], r).start()
            return carry
        lax.fori_loop(0, tm, body, 0, unroll=DMA_ISSUE_UNROLL)

    @pl.when(i == 0)
    def _():
        issue(tok_ref, 0)

    @pl.when(i + 1 < nvb)
    def _():
        issue(tok_next_ref, (i + 1) % 2)

    @pl.when(i < nvb)
    def _():
        slot = i % 2
        _rows_wait(x_hbm, buf_ref.at[slot], sem.at[slot])

        o_ref[...] = buf_ref[slot].astype(o_ref.dtype)

    @pl.when(i >= nvb)
    def _():
        o_ref[...] = jnp.zeros(o_ref.shape, o_ref.dtype)


def _dispatch(hn, row_tok, nvb):
    t, d = hn.shape
    tm = DISPATCH_TM
    n_blocks = row_tok.shape[0] // tm
    tok_spec = lambda ahead: pl.BlockSpec(
        (1, 1, tm), lambda i, nvb_ref: (jnp.minimum(i + ahead, nvb_ref[0] - 1), 0, 0), memory_space=pltpu.SMEM)
    row_tok = row_tok.reshape(n_blocks, 1, tm)
    return pl.pallas_call(
        _dispatch_body,
        grid_spec=pltpu.PrefetchScalarGridSpec(
            num_scalar_prefetch=1,
            grid=(n_blocks,),
            in_specs=[tok_spec(0), tok_spec(1), pl.BlockSpec(memory_space=pl.ANY)],
            out_specs=pl.BlockSpec((tm, d), lambda i, nvb_ref: (i, 0)),
            scratch_shapes=[pltpu.VMEM((2, tm, d), F32), pltpu.SemaphoreType.DMA((2,))]),
        out_shape=jax.ShapeDtypeStruct((n_blocks * tm, d), BF16),
        compiler_params=_params("arbitrary"),
        name="moe_dispatch",
    )(nvb, row_tok, row_tok, hn)


SEG_UNITS = (8, 8, 4, 2, 1)


def _segments(nu):
    n8 = nu >> 3
    b4, b2, b1 = nu & 4, nu & 2, nu & 1
    off4 = n8 * 8
    active = (n8 >= 1, n8 >= 2, b4 != 0, b2 != 0, b1 != 0)
    first = (0, 8, off4, off4 + b4, off4 + b4 + b2)
    return active, first


def _unit_rows(base, first_unit, n_units):
    if isinstance(first_unit, int) and isinstance(base, int):
        return pl.ds(base + first_unit * MOE_TM, n_units * MOE_TM)
    return pl.ds(pl.multiple_of(base + first_unit * MOE_TM, MOE_TM), n_units * MOE_TM)


def _for_units(nu, fn):
    def body(u, carry):
        fn(u)
        return carry
    lax.fori_loop(0, nu, body, 0)


def _process_item(nu_total, compute, reload, load_rows, prepare, produce, out_copy, stage_refs, flag_ref,
                  is_first_step, is_last_step):
    cap = MOE_ITEM_UNITS
    n_pass = (nu_total + cap - 1) >> (cap.bit_length() - 1)

    @pl.when(is_first_step)
    def _():
        for k in range(len(SEG_UNITS)):
            flag_ref[k] = 0

    def one_pass(p, carry):
        nu = jnp.minimum(nu_total - p * cap, cap)
        first_row = p * (cap * MOE_TM)

        @pl.when(compute & (reload | (n_pass > 1)))
        def _():
            load_rows(first_row, nu)

        _stream_segments(nu, compute, prepare, produce,
                         lambda k, first_unit, units, to_item: out_copy(k, first_row, first_unit, units, to_item),
                         stage_refs, flag_ref)
        return carry
    lax.fori_loop(0, n_pass, one_pass, 0)

    @pl.when(is_last_step)
    def _():
        for k, units in enumerate(SEG_UNITS):
            @pl.when(flag_ref[k] == 1)
            def _(k=k, units=units):
                out_copy(k, 0, 0, units, None).wait()
                flag_ref[k] = 0


def _stream_segments(nu, compute, prepare, produce, out_copy, stage_refs, flag_ref):
    active, first = _segments(nu)

    @pl.when(compute & (nu > 0) & jnp.logical_not(active[0]))
    def _():
        prepare()

    for k, (units, stage_ref) in enumerate(zip(SEG_UNITS, stage_refs)):
        @pl.when(active[k])
        def _(k=k, units=units, stage_ref=stage_ref):
            @pl.when(flag_ref[k] == 1)
            def _():
                out_copy(k, 0, units, None).wait()

            @pl.when(compute)
            def _():
                if k == 0:
                    prepare()
                stage_ref[...] = produce(first[k], units).astype(stage_ref.dtype)

            @pl.when(jnp.logical_not(compute))
            def _():
                stage_ref[...] = jnp.zeros(stage_ref.shape, stage_ref.dtype)

            out_copy(k, first[k], units, True).start()
            flag_ref[k] = 1


def _gateup_body(ie_ref, row0_ref, nu_ref, kind_ref, x_hbm, wg_ref, wl_ref, bg_ref, bl_ref, act_hbm,
                 xbuf_ref, wb_ref, st0, st1, st2, st3, st4, flag_ref, sem_x, sem_o, *, last):
    t, j = pl.program_id(0), pl.program_id(1)
    nu_total, r0 = nu_ref[t], row0_ref[t]
    compute = kind_ref[t] == 0
    tn = wg_ref.shape[1]
    stages = (st0, st1, st2, st3, st4)

    def load_rows(first_row, nu):
        def x_copy(u):
            return pltpu.make_async_copy(x_hbm.at[_unit_rows(r0 + first_row, u, 1), :],
                                         xbuf_ref.at[_unit_rows(0, u, 1), :], sem_x)
        _for_units(nu, lambda u: x_copy(u).start())
        _for_units(nu, lambda u: x_copy(u).wait())

    def out_copy(k, first_row, first_unit, units, to_item):
        rows = _unit_rows(r0 + first_row, first_unit, units) if to_item else pl.ds(0, units * MOE_TM)
        return pltpu.make_async_copy(stages[k], act_hbm.at[j, rows, :], sem_o.at[k])

    def prepare():
        wb_ref[:, :tn] = wg_ref[...].astype(BF16)
        wb_ref[:, tn:] = wl_ref[...].astype(BF16)

    def produce(first_unit, units):
        gu = jnp.dot(xbuf_ref[_unit_rows(0, first_unit, units), :], wb_ref[...], preferred_element_type=F32)
        glu = jnp.minimum(gu[:, :tn] + bg_ref[...], SWIGLU_LIMIT)
        lin = jnp.clip(gu[:, tn:] + bl_ref[...], -SWIGLU_LIMIT, SWIGLU_LIMIT)
        return glu * _sigmoid(SWIGLU_ALPHA * glu) * (lin + 1.0)

    _process_item(nu_total, compute, j == 0, load_rows, prepare, produce, out_copy, stages, flag_ref,
                  (t == 0) & (j == 0),
                  (t == last[0]) & (j == last[1]))


def _down_body(ie_ref, row0_ref, nu_ref, kind_ref, a_hbm, w_ref, b_ref, y_hbm,
               abuf_ref, wb_ref, st0, st1, st2, st3, st4, flag_ref, sem_a, sem_o, *, last):
    t, j = pl.program_id(0), pl.program_id(1)
    nu_total, r0 = nu_ref[t], row0_ref[t]
    compute = kind_ref[t] == 0
    n_src, _, src_w = a_hbm.shape
    tn = w_ref.shape[1]
    stages = (st0, st1, st2, st3, st4)

    def load_rows(first_row, nu):
        def a_copy(u, s):
            return pltpu.make_async_copy(a_hbm.at[s, _unit_rows(r0 + first_row, u, 1), :],
                                         abuf_ref.at[_unit_rows(0, u, 1), pl.ds(s * src_w, src_w)], sem_a)

        def start(u):
            for s in range(n_src):
                a_copy(u, s).start()

        def wait(u):
            for s in range(n_src):
                a_copy(u, s).wait()

        _for_units(nu, start)
        _for_units(nu, wait)

    def out_copy(k, first_row, first_unit, units, to_item):
        rows = _unit_rows(r0 + first_row, first_unit, units) if to_item else pl.ds(0, units * MOE_TM)
        cols = pl.ds(pl.multiple_of(j * tn, tn), tn)
        return pltpu.make_async_copy(stages[k], y_hbm.at[rows, cols], sem_o.at[k])

    def prepare():
        wb_ref[...] = w_ref[...].astype(BF16)

    def produce(first_unit, units):
        return jnp.dot(abuf_ref[_unit_rows(0, first_unit, units), :], wb_ref[...],
                       preferred_element_type=F32) + b_ref[...]

    _process_item(nu_total, compute, j == 0, load_rows, prepare, produce, out_copy, stages, flag_ref,
                  (t == 0) & (j == 0),
                  (t == last[0]) & (j == last[1]))


def _item_specs(n_tiles):
    def tile(j, t, nu, kind):
        return jnp.where((kind[t] == 0) & (nu[t] > 0), j, n_tiles - 1)
    return lambda off: (lambda t, j, ie, row0, nu, kind: (ie[t], 0, off + tile(j, t, nu, kind)))


def _moe_scratch(k_dim, wb_cols, tn, stage_dtype):
    return ([pltpu.VMEM((MOE_ITEM_UNITS * MOE_TM, k_dim), BF16), pltpu.VMEM((k_dim, wb_cols), BF16)]
            + [pltpu.VMEM((units * MOE_TM, tn), stage_dtype) for units in SEG_UNITS]
            + [pltpu.SMEM((len(SEG_UNITS),), jnp.int32),
               pltpu.SemaphoreType.DMA(()),
               pltpu.SemaphoreType.DMA((len(SEG_UNITS),))])


def _gateup(xs, w_gate_up, b_gate_up, items):
    rows, d = xs.shape
    ne, _, two_f = w_gate_up.shape
    f = two_f // 2
    tn = MOE_UP_TN
    nt = f // tn
    wmap = _item_specs(nt)
    n_items = items[0].shape[0]
    return pl.pallas_call(
        functools.partial(_gateup_body, last=(n_items - 1, nt - 1)),
        grid_spec=pltpu.PrefetchScalarGridSpec(
            num_scalar_prefetch=4,
            grid=(n_items, nt),
            in_specs=[pl.BlockSpec(memory_space=pl.ANY),
                      pl.BlockSpec((None, d, tn), wmap(0)),
                      pl.BlockSpec((None, d, tn), wmap(nt)),
                      pl.BlockSpec((None, 1, tn), wmap(0)),
                      pl.BlockSpec((None, 1, tn), wmap(nt))],
            out_specs=pl.BlockSpec(memory_space=pl.ANY),
            scratch_shapes=_moe_scratch(d, 2 * tn, tn, BF16)),
        out_shape=jax.ShapeDtypeStruct((nt, rows, tn), BF16),
        compiler_params=_params("arbitrary", "arbitrary"),
        name="moe_gate_up",
    )(*items, xs, w_gate_up, w_gate_up,
      b_gate_up.reshape(ne, 1, two_f), b_gate_up.reshape(ne, 1, two_f))


def _down(act, w_down, b_down, items):
    n_src, rows, src_w = act.shape
    ne, f, d = w_down.shape
    tn = MOE_DOWN_TN
    nt = d // tn
    wmap = _item_specs(nt)
    n_items = items[0].shape[0]
    return pl.pallas_call(
        functools.partial(_down_body, last=(n_items - 1, nt - 1)),
        grid_spec=pltpu.PrefetchScalarGridSpec(
            num_scalar_prefetch=4,
            grid=(n_items, nt),
            in_specs=[pl.BlockSpec(memory_space=pl.ANY),
                      pl.BlockSpec((None, f, tn), wmap(0)),
                      pl.BlockSpec((None, 1, tn), wmap(0))],
            out_specs=pl.BlockSpec(memory_space=pl.ANY),
            scratch_shapes=_moe_scratch(f, tn, tn, F32)),
        out_shape=jax.ShapeDtypeStruct((rows, d), F32),
        compiler_params=_params("arbitrary", "arbitrary"),
        name="moe_down",
    )(*items, act, w_down, b_down.reshape(ne, 1, d))


def _combine_body(dest_ref, dest_next_ref, y_hbm, x_ref, w_ref, g_ref, o_ref, buf_ref, sem, *, n_steps):
    i = pl.program_id(0)
    tb = x_ref.shape[0]

    def issue(idx_ref, slot):
        for k in range(TOP_K):
            def body(t, carry, k=k):
                _row_copy(y_hbm, buf_ref.at[slot, k], sem.at[slot], idx_ref[0, k, t], t).start()
                return carry
            lax.fori_loop(0, tb, body, 0, unroll=DMA_ISSUE_UNROLL)

    @pl.when(i == 0)
    def _():
        issue(dest_ref, 0)

    @pl.when(i + 1 < n_steps)
    def _():
        issue(dest_next_ref, (i + 1) % 2)

    slot = i % 2
    for k in range(TOP_K):
        _rows_wait(y_hbm, buf_ref.at[slot, k], sem.at[slot])

    def group(c, carry):
        sl = pl.ds(pl.multiple_of(c * ROW_GROUP, ROW_GROUP), ROW_GROUP)
        w = w_ref[sl, :]
        y = buf_ref[slot, 0, sl, :] * w[:, 0:1]
        for k in range(1, TOP_K):
            y = y + buf_ref[slot, k, sl, :] * w[:, k:k + 1]
        o_ref[sl, :] = _rms(x_ref[sl, :] + y, g_ref[...])
        return carry
    lax.fori_loop(0, tb // ROW_GROUP, group, 0, unroll=ROW_GROUP_UNROLL)


def _combine(y_rows, x1, dest, top_w, g, tok0, n_tok):
    d = x1.shape[1]
    tb = COMBINE_TB
    blk0, nb = tok0 // tb, n_tok // tb
    dest_b = dest.reshape(TOP_K, -1, tb).transpose(1, 0, 2)
    dest_spec = lambda ahead: pl.BlockSpec(
        (1, TOP_K, tb), lambda i: (blk0 + jnp.minimum(i + ahead, nb - 1), 0, 0), memory_space=pltpu.SMEM)
    return pl.pallas_call(
        functools.partial(_combine_body, n_steps=nb),
        grid=(nb,),
        in_specs=[dest_spec(0), dest_spec(1),
                  pl.BlockSpec(memory_space=pl.ANY),
                  pl.BlockSpec((tb, d), lambda i: (blk0 + i, 0)),
                  pl.BlockSpec((tb, TOP_K), lambda i: (blk0 + i, 0)),
                  pl.BlockSpec((1, d), lambda i: (0, 0))],
        out_specs=pl.BlockSpec((tb, d), lambda i: (i, 0)),
        out_shape=jax.ShapeDtypeStruct((n_tok, d), F32),
        scratch_shapes=[pltpu.VMEM((2, TOP_K, tb, d), F32), pltpu.SemaphoreType.DMA((2,))],
        compiler_params=_params("arbitrary"),
        name="moe_combine_prompt" if tok0 == 0 else "moe_combine_sample",
    )(dest_b, dest_b, y_rows, x1, top_w.T, g.reshape(1, d))


def _pick(table, idx):
    hot = idx[..., None] == jnp.arange(table.shape[0], dtype=idx.dtype)
    return jnp.sum(jnp.where(hot, table, 0), axis=-1)


def _plan(top_e, rank, counts, n_units):
    u = MOE_TM
    n_tok = top_e.shape[1]
    i32 = jnp.int32
    units = (counts + u - 1) // u
    uend = jnp.cumsum(units)
    ustart = uend - units
    nvu = uend[-1]
    dest = _pick(ustart * u, top_e) + rank
    tok = jnp.broadcast_to(jnp.arange(n_tok, dtype=i32)[None, :], dest.shape)
    row_tok = jnp.zeros((n_units * u,), i32).at[dest.reshape(-1)].set(tok.reshape(-1))
    nvb = (nvu * u + DISPATCH_TM - 1) // DISPATCH_TM

    ne = counts.shape[0]
    tail = jnp.full((1,), 1, i32)
    busy_e = lax.cummax(jnp.where(units > 0, jnp.arange(ne, dtype=i32), 0))
    item_e = jnp.concatenate([busy_e, busy_e[-1:]])
    row0 = jnp.concatenate([ustart * u, tail * (nvu * u)])
    nu = jnp.concatenate([units, tail * (n_units - nvu)])
    kind = jnp.concatenate([jnp.zeros((ne,), i32), tail])
    items = tuple(a.astype(i32) for a in (item_e, row0, nu, kind))
    return dest, row_tok, nvb.astype(i32).reshape(1), items


def kernel(x_prompt, x_sample, cache_attn_k, cache_attn_v, state_ret, norm_mix, w_in, attn_sinks,
           w_o_attn, w_o_ret, w_out, norm_ffn, w_router, b_router, w_gate_up, b_gate_up,
           w_down, b_down, norm_final):
    bp, sp, d = x_prompt.shape
    bs, ss, _ = x_sample.shape
    depth = w_in.shape[0]
    assert depth == 1 and ss == CHUNK and sp % CHUNK == 0
    tp, ts = bp * sp, bs * ss
    t = tp + ts
    ncp = sp // CHUNK

    widths = (ATTN_Q_W, ATTN_KV_W, ATTN_KV_W, RET_W, RET_W, RET_W, RET_W, d, d)
    cols = [0]
    for wdt in widths:
        cols.append(cols[-1] + wdt)
    c_aq, c_ak, c_av, c_rq, c_rk, c_rv, c_rg, c_ga, c_gr = cols[:9]

    x = (x_prompt.reshape(tp, d), x_sample.reshape(ts, d))
    h = _rmsnorm(x[0], x[1], norm_mix[0], BF16)
    z = _matmul(h, w_in[0], F32, "in_proj")

    bias = _attn_bias(attn_sinks[0])
    n_win = cache_attn_k.shape[2]
    ck = cache_attn_k[0].reshape(bs * n_win, ATTN_KV_W)
    cv = cache_attn_v[0].reshape(bs * n_win, ATTN_KV_W)
    attn_p = _attention(z, bias, 0, bp, ncp, c_ak, c_av)
    attn_s = _attention(z, bias, tp // CHUNK, bs, 1, c_ak, c_av, ck, cv)

    consts = _ret_consts()
    r_p, state_p = _retention(z, consts, 0, bp, ncp, c_rq, c_rk, c_rv, c_rg)
    r_s, state_s = _retention(z, consts, tp // CHUNK, bs, 1, c_rq, c_rk, c_rv, c_rg, state_ret[0])

    merged = _merge((attn_p, attn_s), (r_p, r_s), w_o_attn[0], w_o_ret[0], z, c_ga, c_gr)
    x1 = _matmul(merged, w_out[0], F32, "out_proj", residual=x)

    hn, top_e, top_w, rank, cnt = _router(x1, norm_ffn[0], w_router[0], b_router[0])
    n_units = -(-(t * TOP_K) // MOE_TM) + N_EXPERTS
    n_units += -n_units % (DISPATCH_TM // MOE_TM)
    dest, row_tok, nvb, items = _plan(top_e, rank, cnt[:, 0], n_units)
    xs = _dispatch(hn, row_tok, nvb)
    act = _gateup(xs, w_gate_up[0], b_gate_up[0], items)
    y_rows = _down(act, w_down[0], b_down[0], items)
    y_prompt = _combine(y_rows, x1, dest, top_w, norm_final, 0, tp).reshape(bp, sp, d)
    y_sample = _combine(y_rows, x1, dest, top_w, norm_final, tp, ts).reshape(bs, ss, d)

    n_keep = min(WINDOW, sp)
    kv = z[:, c_ak:c_ak + 2 * ATTN_KV_W]
    kv_p = kv[:tp].reshape(bp, sp, 2, ATTN_KV_HEADS, ATTN_HEAD_DIM)[:, sp - n_keep:]
    kv_s = kv[tp:].reshape(bs, ss, 2, ATTN_KV_HEADS, ATTN_HEAD_DIM)
    return (y_prompt, y_sample, kv_p[:, :, 0][None], kv_p[:, :, 1][None], state_p[None],
            kv_s[:, :, 0][None], kv_s[:, :, 1][None], state_s[None])
```

```python
import functools

import jax
import jax.numpy as jnp
from jax import lax
from jax.experimental import pallas as pl
from jax.experimental.pallas import tpu as pltpu

F32 = jnp.float32
BF16 = jnp.bfloat16

CHUNK = 64
WINDOW = 128
ATTN_HEADS = 32
ATTN_KV_HEADS = 4
ATTN_GROUP = ATTN_HEADS // ATTN_KV_HEADS
ATTN_HEAD_DIM = 64
ATTN_Q_W = ATTN_HEADS * ATTN_HEAD_DIM
ATTN_KV_W = ATTN_KV_HEADS * ATTN_HEAD_DIM
RET_HEADS = 8
RET_HEAD_DIM = 256
RET_W = RET_HEADS * RET_HEAD_DIM
N_EXPERTS = 32
TOP_K = 4
SWIGLU_ALPHA = 1.702
SWIGLU_LIMIT = 7.0
RMS_EPS = 1e-5
NEG_INF = -1e30

ATTN_COLS = 256
RET_HEADS_PER_STEP = 2
RET_SEQS_PER_STEP = 4

VMEM_LIMIT_BYTES = 56 * 1024 * 1024

MOE_TM = 128
MOE_ITEM_UNITS = 16
DISPATCH_TM = 512
MOE_UP_TN = 256
MOE_DOWN_TN = 512
PROJ_TM = 1024
PROJ_TN = 512
MERGE_TM = 512
COMBINE_TB = 128
ROUTER_TB = 256
NORM_TM = 256


def _params(*semantics):
    return pltpu.CompilerParams(dimension_semantics=semantics, vmem_limit_bytes=VMEM_LIMIT_BYTES)


def _rms(x, g):
    ms = jnp.mean(x * x, axis=-1, keepdims=True)
    return x * lax.rsqrt(ms + RMS_EPS) * g


def _sigmoid(x):
    return 1.0 / (1.0 + jnp.exp(-x))


def _first(i, nf):
    return jnp.minimum(i, nf - 1)


def _second(i, nf):
    return jnp.maximum(i - nf, 0)


def _on_rows(i, nf, fn, first_refs, second_refs):
    @pl.when(i < nf)
    def _():
        fn(*first_refs)

    @pl.when(i >= nf)
    def _():
        fn(*second_refs)


def _rmsnorm_body(xp_ref, xs_ref, g_ref, o_ref, *, nf):
    def norm(x_ref):
        o_ref[...] = _rms(x_ref[...], g_ref[...]).astype(o_ref.dtype)

    _on_rows(pl.program_id(0), nf, norm, (xp_ref,), (xs_ref,))


def _rmsnorm(xp, xs, g, out_dtype):
    (mp, d), ms = xp.shape, xs.shape[0]
    tm = min(NORM_TM, ms)
    nf = mp // tm
    return pl.pallas_call(
        functools.partial(_rmsnorm_body, nf=nf),
        grid=((mp + ms) // tm,),
        in_specs=[pl.BlockSpec((tm, d), lambda i: (_first(i, nf), 0)),
                  pl.BlockSpec((tm, d), lambda i: (_second(i, nf), 0)),
                  pl.BlockSpec((1, d), lambda i: (0, 0))],
        out_specs=pl.BlockSpec((tm, d), lambda i: (i, 0)),
        out_shape=jax.ShapeDtypeStruct((mp + ms, d), out_dtype),
        compiler_params=_params("arbitrary"),
        name="rmsnorm",
    )(xp, xs, g.reshape(1, d))


def _matmul_body(x_ref, w_ref, *rest, nf):
    if nf is None:
        o_ref, wb_ref = rest
    else:
        rp_ref, rs_ref, o_ref, wb_ref = rest
    i = pl.program_id(1)

    @pl.when(i == 0)
    def _():
        wb_ref[...] = w_ref[...].astype(BF16)

    acc = jnp.dot(x_ref[...], wb_ref[...], preferred_element_type=F32)
    if nf is None:
        o_ref[...] = acc.astype(o_ref.dtype)
    else:
        def add(r_ref):
            o_ref[...] = (acc + r_ref[...]).astype(o_ref.dtype)

        _on_rows(i, nf, add, (rp_ref,), (rs_ref,))


def _matmul(x, w, out_dtype, name, residual=None):
    m, k = x.shape
    n = w.shape[1]
    tn = min(PROJ_TN, n)
    in_specs = [None, pl.BlockSpec((k, tn), lambda j, i: (0, j))]
    args = [x, w]
    nf = None
    if residual is None:
        tm = min(PROJ_TM, m)
    else:
        tm = min(PROJ_TM, residual[1].shape[0])
        nf = residual[0].shape[0] // tm
        in_specs += [pl.BlockSpec((tm, tn), lambda j, i: (_first(i, nf), j)),
                     pl.BlockSpec((tm, tn), lambda j, i: (_second(i, nf), j))]
        args += list(residual)
    in_specs[0] = pl.BlockSpec((tm, k), lambda j, i: (i, 0))
    return pl.pallas_call(
        functools.partial(_matmul_body, nf=nf),
        grid=(n // tn, m // tm),
        in_specs=in_specs,
        out_specs=pl.BlockSpec((tm, tn), lambda j, i: (i, j)),
        out_shape=jax.ShapeDtypeStruct((m, n), out_dtype),
        scratch_shapes=[pltpu.VMEM((k, tn), BF16)],
        compiler_params=_params("arbitrary", "arbitrary"),
        name=name,
    )(*args)


def _merge_body(ap_ref, as_ref, rp_ref, rs_ref, wa_ref, wr_ref, ga_ref, gr_ref, o_ref, wab_ref, wrb_ref, *, nf):
    i = pl.program_id(1)

    @pl.when(i == 0)
    def _():
        wab_ref[...] = wa_ref[...].astype(BF16)
        wrb_ref[...] = wr_ref[...].astype(BF16)

    def merge(a_ref, r_ref):
        ya = jnp.dot(a_ref[...], wab_ref[...], preferred_element_type=F32)
        yr = jnp.dot(r_ref[...], wrb_ref[...], preferred_element_type=F32)
        o_ref[...] = (_sigmoid(ga_ref[...]) * ya + _sigmoid(gr_ref[...]) * yr).astype(o_ref.dtype)

    _on_rows(i, nf, merge, (ap_ref, rp_ref), (as_ref, rs_ref))


def _merge(a, r, w_a, w_r, z, ga_col, gr_col):
    (mp, ka), ms = a[0].shape, a[1].shape[0]
    kr = r[0].shape[1]
    n = w_a.shape[1]
    tm, tn = min(MERGE_TM, ms), min(PROJ_TN, n)
    nf = mp // tm
    ga_blk, gr_blk = ga_col // tn, gr_col // tn
    rows = lambda k, pick: pl.BlockSpec((tm, k), lambda j, i: (pick(i, nf), 0))
    return pl.pallas_call(
        functools.partial(_merge_body, nf=nf),
        grid=(n // tn, (mp + ms) // tm),
        in_specs=[rows(ka, _first), rows(ka, _second), rows(kr, _first), rows(kr, _second),
                  pl.BlockSpec((ka, tn), lambda j, i: (0, j)),
                  pl.BlockSpec((kr, tn), lambda j, i: (0, j)),
                  pl.BlockSpec((tm, tn), lambda j, i: (i, ga_blk + j)),
                  pl.BlockSpec((tm, tn), lambda j, i: (i, gr_blk + j))],
        out_specs=pl.BlockSpec((tm, tn), lambda j, i: (i, j)),
        out_shape=jax.ShapeDtypeStruct((mp + ms, n), BF16),
        scratch_shapes=[pltpu.VMEM((ka, tn), BF16), pltpu.VMEM((kr, tn), BF16)],
        compiler_params=_params("arbitrary", "arbitrary"),
        name="gated_merge",
    )(a[0], a[1], r[0], r[1], w_a, w_r, z, z)


def _attn_body(q_ref, k0_ref, k1_ref, k2_ref, v0_ref, v1_ref, v2_ref, bias_ref, o_ref, *, prompt):
    hd, grp = ATTN_HEAD_DIM, ATTN_GROUP
    rows = grp * CHUNK
    n_keys = WINDOW + CHUNK
    pad = jnp.zeros((ATTN_COLS - n_keys, hd), F32)
    if prompt:
        n_invalid = jnp.maximum(WINDOW - pl.program_id(1) * CHUNK, 0)
        col = lax.broadcasted_iota(jnp.int32, (rows, ATTN_COLS), 1)
    for h in range(ATTN_KV_HEADS):
        sl = slice(h * hd, (h + 1) * hd)
        kh = jnp.concatenate([k0_ref[:, sl], k1_ref[:, sl], k2_ref[:, sl], pad], axis=0).astype(BF16)
        vh = jnp.concatenate([v0_ref[:, sl], v1_ref[:, sl], v2_ref[:, sl], pad], axis=0).astype(BF16)
        qh = jnp.concatenate(
            [q_ref[:, (h * grp + g) * hd:(h * grp + g + 1) * hd] for g in range(grp)], axis=0).astype(BF16)
        s = lax.dot_general(qh, kh, (((1,), (1,)), ((), ())), preferred_element_type=F32)
        s = s * (hd ** -0.5) + bias_ref[h * rows:(h + 1) * rows, :]
        if prompt:
            s = jnp.where(col < n_invalid, NEG_INF, s)
        m = jnp.max(s, axis=-1, keepdims=True)
        p = jnp.exp(s - m)
        p = (p / jnp.sum(p, axis=-1, keepdims=True)).astype(BF16)
        oh = jnp.dot(p, vh, preferred_element_type=F32)
        for g in range(grp):
            off = (h * grp + g) * hd
            o_ref[:, off:off + hd] = oh[g * CHUNK:(g + 1) * CHUNK, :].astype(o_ref.dtype)


def _attn_bias(sinks):
    n_keys = WINDOW + CHUNK
    slopes = 2.0 ** (-8.0 * jnp.arange(1, ATTN_HEADS + 1, dtype=F32) / ATTN_HEADS)
    q_pos = WINDOW + jnp.arange(CHUNK)
    k_pos = jnp.arange(n_keys)
    dist = jnp.abs(q_pos[:, None] - k_pos[None, :]).astype(F32)
    alibi = -(slopes[:, None, None] * dist[None])
    sink = jnp.broadcast_to(sinks.astype(F32)[:, None, None], (ATTN_HEADS, CHUNK, 1))
    fill = jnp.full((ATTN_HEADS, CHUNK, ATTN_COLS - n_keys - 1), NEG_INF, F32)
    return jnp.concatenate([alibi, sink, fill], axis=-1).reshape(ATTN_HEADS * CHUNK, ATTN_COLS)


def _attention(z, bias, row_blk0, n_seq, n_chunks, k_col, v_col, cache_k=None, cache_v=None):
    kb, vb = k_col // ATTN_KV_W, v_col // ATTN_KV_W
    wc = WINDOW // CHUNK

    def q_map(b, c):
        return (row_blk0 + b * n_chunks + c, 0)

    if cache_k is None:
        def kv_map(j, colblk):
            return lambda b, c: (row_blk0 + b * n_chunks + jnp.maximum(c - wc + j, 0), colblk)
        kv_args = [z] * 6
        kv_specs = [pl.BlockSpec((CHUNK, ATTN_KV_W), kv_map(j, cb)) for cb in (kb, vb) for j in range(wc + 1)]
    else:
        def cache_map(j):
            return lambda b, c: (b * wc + j, 0)
        def new_map(colblk):
            return lambda b, c: (row_blk0 + b, colblk)
        kv_args = [cache_k, cache_k, z, cache_v, cache_v, z]
        kv_specs = []
        for cb in (kb, vb):
            kv_specs += [pl.BlockSpec((CHUNK, ATTN_KV_W), cache_map(j)) for j in range(wc)]
            kv_specs.append(pl.BlockSpec((CHUNK, ATTN_KV_W), new_map(cb)))
    return pl.pallas_call(
        functools.partial(_attn_body, prompt=cache_k is None),
        grid=(n_seq, n_chunks),
        in_specs=[pl.BlockSpec((CHUNK, ATTN_Q_W), q_map)] + kv_specs
                 + [pl.BlockSpec(bias.shape, lambda b, c: (0, 0))],
        out_specs=pl.BlockSpec((CHUNK, ATTN_Q_W), lambda b, c: (b * n_chunks + c, 0)),
        out_shape=jax.ShapeDtypeStruct((n_seq * n_chunks * CHUNK, ATTN_Q_W), BF16),
        compiler_params=_params("arbitrary", "arbitrary"),
        name="attn_prompt" if cache_k is None else "attn_sample",
    )(z, *kv_args, bias)


def _ret_body(*refs, has_init, n_chunks):
    sps = RET_SEQS_PER_STEP
    z_refs = refs[:4 * sps]
    decay_ref, dout_ref, kd_ref, g64_ref = refs[4 * sps:4 * sps + 4]
    rest = refs[4 * sps + 4:]
    if has_init:
        s0_ref, r_ref, sout_ref, state_ref = rest
    else:
        r_ref, sout_ref, state_ref = rest
    c = pl.program_id(2)
    hd = RET_HEAD_DIM

    @pl.when(c == 0)
    def _():
        if has_init:
            state_ref[...] = s0_ref[...]
        else:
            state_ref[...] = jnp.zeros(state_ref.shape, F32)

    for s in range(sps):
        q_ref, k_ref, v_ref, g_ref = z_refs[4 * s:4 * s + 4]
        for hh in range(RET_HEADS_PER_STEP):
            sl = slice(hh * hd, (hh + 1) * hd)
            q = q_ref[:, sl].astype(BF16)
            k = k_ref[:, sl] * (hd ** -0.5)
            v = v_ref[:, sl].astype(BF16)
            st = state_ref[s, hh]
            scores = lax.dot_general(q, k.astype(BF16), (((1,), (1,)), ((), ())),
                                     preferred_element_type=F32) * decay_ref[hh]
            o = jnp.dot(scores.astype(BF16), v, preferred_element_type=F32)
            o = o + jnp.dot(q, st.astype(BF16), preferred_element_type=F32) * dout_ref[hh]
            k_dec = (k * kd_ref[hh]).T.astype(BF16)
            state_ref[s, hh] = st * g64_ref[hh] + jnp.dot(k_dec, v, preferred_element_type=F32)
            rn = o * lax.rsqrt(jnp.mean(o * o, axis=-1, keepdims=True) + RMS_EPS)
            gate = g_ref[:, sl]
            r_ref[s, :, sl] = (rn * (gate * _sigmoid(gate))).astype(r_ref.dtype)

    @pl.when(c == n_chunks - 1)
    def _():
        sout_ref[...] = state_ref[...]


def _ret_consts():
    log_g = jnp.log1p(-(2.0 ** (-5.0 - jnp.arange(RET_HEADS, dtype=F32))))
    idx = jnp.arange(CHUNK, dtype=F32)
    rel = idx[:, None] - idx[None, :]
    decay = jnp.where(rel >= 0, jnp.exp(log_g[:, None, None] * jnp.maximum(rel, 0.0)), 0.0)
    dout = jnp.exp((idx[None, :] + 1.0) * log_g[:, None])
    kd = jnp.exp((CHUNK - 1.0 - idx)[None, :] * log_g[:, None])
    g64 = jnp.exp(CHUNK * log_g)
    bc = lambda a: jnp.broadcast_to(a[:, :, None], (RET_HEADS, a.shape[1], RET_HEAD_DIM))
    return decay, bc(dout), bc(kd), bc(g64[:, None])


def _retention(z, consts, row_blk0, n_seq, n_chunks, q_col, k_col, v_col, g_col, state0=None):
    hps, sps = RET_HEADS_PER_STEP, RET_SEQS_PER_STEP
    w = hps * RET_HEAD_DIM
    n_hp = RET_HEADS // hps
    decay, dout, kd, g64 = consts

    def zmap(s, col):
        cb = col // w
        return lambda b, h, c: (row_blk0 + (b * sps + s) * n_chunks + c, cb + h)

    cmap = lambda b, h, c: (h, 0, 0)
    in_specs = [pl.BlockSpec((CHUNK, w), zmap(s, col)) for s in range(sps) for col in (q_col, k_col, v_col, g_col)]
    in_specs += [pl.BlockSpec((hps, CHUNK, CHUNK), cmap),
                 pl.BlockSpec((hps, CHUNK, RET_HEAD_DIM), cmap),
                 pl.BlockSpec((hps, CHUNK, RET_HEAD_DIM), cmap),
                 pl.BlockSpec((hps, 1, RET_HEAD_DIM), cmap)]
    args = [z] * (4 * sps) + [decay, dout, kd, g64]
    st_spec = pl.BlockSpec((sps, hps, RET_HEAD_DIM, RET_HEAD_DIM), lambda b, h, c: (b, h, 0, 0))
    if state0 is not None:
        in_specs.append(st_spec)
        args.append(state0)
    r, state = pl.pallas_call(
        functools.partial(_ret_body, has_init=state0 is not None, n_chunks=n_chunks),
        grid=(n_seq // sps, n_hp, n_chunks),
        in_specs=in_specs,
        out_specs=[pl.BlockSpec((sps, CHUNK, w), lambda b, h, c: (b, c, h)), st_spec],
        out_shape=[jax.ShapeDtypeStruct((n_seq, n_chunks * CHUNK, RET_W), BF16),
                   jax.ShapeDtypeStruct((n_seq, RET_HEADS, RET_HEAD_DIM, RET_HEAD_DIM), F32)],
        scratch_shapes=[pltpu.VMEM((sps, hps, RET_HEAD_DIM, RET_HEAD_DIM), F32)],
        compiler_params=_params("arbitrary", "arbitrary", "arbitrary"),
        name="retention_prompt" if state0 is None else "retention_sample",
    )(*args)
    return r.reshape(n_seq * n_chunks * CHUNK, RET_W), state


def _router_body(x_ref, g_ref, whi_ref, wlo_ref, b_ref, tri_ref, hn_ref, e_ref, w_ref, rank_ref, cnt_ref,
                 carry_ref):
    ne, tb = b_ref.shape

    @pl.when(pl.program_id(0) == 0)
    def _():
        carry_ref[...] = jnp.zeros(carry_ref.shape, F32)

    hn = _rms(x_ref[...], g_ref[...])
    hn_ref[...] = hn
    hi = hn.astype(BF16)
    lo = (hn - hi.astype(F32)).astype(BF16)
    dn = (((1,), (1,)), ((), ()))
    logits = (lax.dot_general(whi_ref[...], hi, dn, preferred_element_type=F32)
              + lax.dot_general(whi_ref[...], lo, dn, preferred_element_type=F32)
              + lax.dot_general(wlo_ref[...], hi, dn, preferred_element_type=F32)) + b_ref[...]
    eidx = lax.broadcasted_iota(jnp.int32, (ne, tb), 0)
    rest = logits
    tops, idxs, hots = [], [], []
    for _ in range(TOP_K):
        m = jnp.max(rest, axis=0, keepdims=True)
        idx = jnp.min(jnp.where(rest == m, eidx, ne), axis=0, keepdims=True)
        hot = eidx == idx
        tops.append(m)
        idxs.append(idx)
        hots.append(hot)
        rest = jnp.where(hot, -jnp.inf, rest)
    ex = [jnp.exp(t - tops[0]) for t in tops]
    den = ex[0] + ex[1] + ex[2] + ex[3]
    member = jnp.zeros((ne, tb), F32)
    for hot in hots:
        member = member + jnp.where(hot, 1.0, 0.0)
    before = jnp.dot(member.astype(BF16), tri_ref[...], preferred_element_type=F32)
    carry = carry_ref[...]
    base = before + jnp.concatenate([carry] * (tb // carry.shape[1]), axis=1)
    ranks = [jnp.sum(jnp.where(hot, base, 0.0), axis=0, keepdims=True) for hot in hots]
    e_ref[...] = jnp.concatenate(idxs, axis=0)
    w_ref[...] = jnp.concatenate([e / den for e in ex], axis=0)
    rank_ref[...] = jnp.concatenate(ranks, axis=0).astype(jnp.int32)
    carry = carry + jnp.sum(member, axis=1, keepdims=True)
    carry_ref[...] = carry
    cnt_ref[...] = carry.astype(jnp.int32)


def _router(x1, g, w_router, b_router):
    t, d = x1.shape
    ne = w_router.shape[1]
    tb = ROUTER_TB
    wt = w_router.T
    wt_hi = wt.astype(BF16)
    wt_lo = (wt - wt_hi.astype(F32)).astype(BF16)
    bias = jnp.broadcast_to(b_router.astype(F32)[:, None], (ne, tb))
    tri = (jnp.arange(tb)[:, None] < jnp.arange(tb)[None, :]).astype(BF16)
    const = lambda shape: pl.BlockSpec(shape, lambda i: (0,) * len(shape))
    col = pl.BlockSpec((TOP_K, tb), lambda i: (0, i))
    return pl.pallas_call(
        _router_body,
        grid=(t // tb,),
        in_specs=[pl.BlockSpec((tb, d), lambda i: (i, 0)), const((1, d)), const((ne, d)), const((ne, d)),
                  const((ne, tb)), const((tb, tb))],
        out_specs=[pl.BlockSpec((tb, d), lambda i: (i, 0)), col, col, col, const((ne, 128))],
        out_shape=[jax.ShapeDtypeStruct((t, d), F32),
                   jax.ShapeDtypeStruct((TOP_K, t), jnp.int32),
                   jax.ShapeDtypeStruct((TOP_K, t), F32),
                   jax.ShapeDtypeStruct((TOP_K, t), jnp.int32),
                   jax.ShapeDtypeStruct((ne, 128), jnp.int32)],
        scratch_shapes=[pltpu.VMEM((ne, 128), F32)],
        compiler_params=_params("arbitrary"),
        name="router",
    )(x1, g.reshape(1, d), wt_hi, wt_lo, bias, tri)


DMA_ISSUE_UNROLL = 8
ROW_GROUP = 8
ROW_GROUP_UNROLL = 4


def _row_copy(src_hbm, dst, sem, src_row, dst_row):
    return pltpu.make_async_copy(src_hbm.at[pl.ds(src_row, 1), :], dst.at[pl.ds(dst_row, 1), :], sem)


def _rows_wait(src_hbm, dst, sem):
    pltpu.make_async_copy(src_hbm.at[pl.ds(0, dst.shape[0]), :], dst, sem).wait()


def _dispatch_body(nvb_ref, tok_ref, tok_next_ref, x_hbm, o_ref, buf_ref, sem):
    i = pl.program_id(0)
    nvb = nvb_ref[0]
    tm = buf_ref.shape[1]

    def issue(idx_ref, slot):
        def body(r, carry):
            _row_copy(x_hbm, buf_ref.at[slot], sem.at[slot], idx_ref[0, 0, r], r).start()
            return carry
        lax.fori_loop(0, tm, body, 0, unroll=DMA_ISSUE_UNROLL)

    @pl.when(i == 0)
    def _():
        issue(tok_ref, 0)

    @pl.when(i + 1 < nvb)
    def _():
        issue(tok_next_ref, (i + 1) % 2)

    @pl.when(i < nvb)
    def _():
        slot = i % 2
        _rows_wait(x_hbm, buf_ref.at[slot], sem.at[slot])

        o_ref[...] = buf_ref[slot].astype(o_ref.dtype)

    @pl.when(i >= nvb)
    def _():
        o_ref[...] = jnp.zeros(o_ref.shape, o_ref.dtype)


def _dispatch(hn, row_tok, nvb):
    t, d = hn.shape
    tm = DISPATCH_TM
    n_blocks = row_tok.shape[0] // tm
    tok_spec = lambda ahead: pl.BlockSpec(
        (1, 1, tm), lambda i, nvb_ref: (jnp.minimum(i + ahead, nvb_ref[0] - 1), 0, 0), memory_space=pltpu.SMEM)
    row_tok = row_tok.reshape(n_blocks, 1, tm)
    return pl.pallas_call(
        _dispatch_body,
        grid_spec=pltpu.PrefetchScalarGridSpec(
            num_scalar_prefetch=1,
            grid=(n_blocks,),
            in_specs=[tok_spec(0), tok_spec(1), pl.BlockSpec(memory_space=pl.ANY)],
            out_specs=pl.BlockSpec((tm, d), lambda i, nvb_ref: (i, 0)),
            scratch_shapes=[pltpu.VMEM((2, tm, d), F32), pltpu.SemaphoreType.DMA((2,))]),
        out_shape=jax.ShapeDtypeStruct((n_blocks * tm, d), BF16),
        compiler_params=_params("arbitrary"),
        name="moe_dispatch",
    )(nvb, row_tok, row_tok, hn)


SEG_UNITS = (8, 8, 4, 2, 1)


def _segments(nu):
    n8 = nu >> 3
    b4, b2, b1 = nu & 4, nu & 2, nu & 1
    off4 = n8 * 8
    active = (n8 >= 1, n8 >= 2, b4 != 0, b2 != 0, b1 != 0)
    first = (0, 8, off4, off4 + b4, off4 + b4 + b2)
    return active, first


def _unit_rows(base, first_unit, n_units):
    if isinstance(first_unit, int) and isinstance(base, int):
        return pl.ds(base + first_unit * MOE_TM, n_units * MOE_TM)
    return pl.ds(pl.multiple_of(base + first_unit * MOE_TM, MOE_TM), n_units * MOE_TM)


def _for_units(nu, fn):
    def body(u, carry):
        fn(u)
        return carry
    lax.fori_loop(0, nu, body, 0)


def _process_item(nu_total, compute, reload, load_rows, prepare, produce, out_copy, stage_refs, flag_ref,
                  is_first_step, is_last_step):
    cap = MOE_ITEM_UNITS
    n_pass = (nu_total + cap - 1) >> (cap.bit_length() - 1)

    @pl.when(is_first_step)
    def _():
        for k in range(len(SEG_UNITS)):
            flag_ref[k] = 0

    def one_pass(p, carry):
        nu = jnp.minimum(nu_total - p * cap, cap)
        first_row = p * (cap * MOE_TM)

        @pl.when(compute & (reload | (n_pass > 1)))
        def _():
            load_rows(first_row, nu)

        _stream_segments(nu, compute, prepare, produce,
                         lambda k, first_unit, units, to_item: out_copy(k, first_row, first_unit, units, to_item),
                         stage_refs, flag_ref)
        return carry
    lax.fori_loop(0, n_pass, one_pass, 0)

    @pl.when(is_last_step)
    def _():
        for k, units in enumerate(SEG_UNITS):
            @pl.when(flag_ref[k] == 1)
            def _(k=k, units=units):
                out_copy(k, 0, 0, units, None).wait()
                flag_ref[k] = 0


def _stream_segments(nu, compute, prepare, produce, out_copy, stage_refs, flag_ref):
    active, first = _segments(nu)

    @pl.when(compute & (nu > 0) & jnp.logical_not(active[0]))
    def _():
        prepare()

    for k, (units, stage_ref) in enumerate(zip(SEG_UNITS, stage_refs)):
        @pl.when(active[k])
        def _(k=k, units=units, stage_ref=stage_ref):
            @pl.when(flag_ref[k] == 1)
            def _():
                out_copy(k, 0, units, None).wait()

            @pl.when(compute)
            def _():
                if k == 0:
                    prepare()
                stage_ref[...] = produce(first[k], units).astype(stage_ref.dtype)

            @pl.when(jnp.logical_not(compute))
            def _():
                stage_ref[...] = jnp.zeros(stage_ref.shape, stage_ref.dtype)

            out_copy(k, first[k], units, True).start()
            flag_ref[k] = 1


def _gateup_body(ie_ref, row0_ref, nu_ref, kind_ref, x_hbm, wg_ref, wl_ref, bg_ref, bl_ref, act_hbm,
                 xbuf_ref, wb_ref, st0, st1, st2, st3, st4, flag_ref, sem_x, sem_o, *, last):
    t, j = pl.program_id(0), pl.program_id(1)
    nu_total, r0 = nu_ref[t], row0_ref[t]
    compute = kind_ref[t] == 0
    tn = wg_ref.shape[1]
    stages = (st0, st1, st2, st3, st4)

    def load_rows(first_row, nu):
        def x_copy(u):
            return pltpu.make_async_copy(x_hbm.at[_unit_rows(r0 + first_row, u, 1), :],
                                         xbuf_ref.at[_unit_rows(0, u, 1), :], sem_x)
        _for_units(nu, lambda u: x_copy(u).start())
        _for_units(nu, lambda u: x_copy(u).wait())

    def out_copy(k, first_row, first_unit, units, to_item):
        rows = _unit_rows(r0 + first_row, first_unit, units) if to_item else pl.ds(0, units * MOE_TM)
        cols = pl.ds(pl.multiple_of(j * tn, tn), tn)
        return pltpu.make_async_copy(stages[k], act_hbm.at[rows, cols], sem_o.at[k])

    def prepare():
        wb_ref[:, :tn] = wg_ref[...].astype(BF16)
        wb_ref[:, tn:] = wl_ref[...].astype(BF16)

    def produce(first_unit, units):
        gu = jnp.dot(xbuf_ref[_unit_rows(0, first_unit, units), :], wb_ref[...], preferred_element_type=F32)
        glu = jnp.minimum(gu[:, :tn] + bg_ref[...], SWIGLU_LIMIT)
        lin = jnp.clip(gu[:, tn:] + bl_ref[...], -SWIGLU_LIMIT, SWIGLU_LIMIT)
        return glu * _sigmoid(SWIGLU_ALPHA * glu) * (lin + 1.0)

    _process_item(nu_total, compute, j == 0, load_rows, prepare, produce, out_copy, stages, flag_ref,
                  (t == 0) & (j == 0),
                  (t == last[0]) & (j == last[1]))


def _down_body(ie_ref, row0_ref, nu_ref, kind_ref, a_hbm, w_ref, b_ref, y_hbm,
               abuf_ref, wb_ref, st0, st1, st2, st3, st4, flag_ref, sem_a, sem_o, *, last):
    t, j = pl.program_id(0), pl.program_id(1)
    nu_total, r0 = nu_ref[t], row0_ref[t]
    compute = kind_ref[t] == 0
    tn = w_ref.shape[1]
    stages = (st0, st1, st2, st3, st4)

    def load_rows(first_row, nu):
        def a_copy(u):
            return pltpu.make_async_copy(a_hbm.at[_unit_rows(r0 + first_row, u, 1), :],
                                         abuf_ref.at[_unit_rows(0, u, 1), :], sem_a)
        _for_units(nu, lambda u: a_copy(u).start())
        _for_units(nu, lambda u: a_copy(u).wait())

    def out_copy(k, first_row, first_unit, units, to_item):
        rows = _unit_rows(r0 + first_row, first_unit, units) if to_item else pl.ds(0, units * MOE_TM)
        cols = pl.ds(pl.multiple_of(j * tn, tn), tn)
        return pltpu.make_async_copy(stages[k], y_hbm.at[rows, cols], sem_o.at[k])

    def prepare():
        wb_ref[...] = w_ref[...].astype(BF16)

    def produce(first_unit, units):
        return jnp.dot(abuf_ref[_unit_rows(0, first_unit, units), :], wb_ref[...],
                       preferred_element_type=F32) + b_ref[...]

    _process_item(nu_total, compute, j == 0, load_rows, prepare, produce, out_copy, stages, flag_ref,
                  (t == 0) & (j == 0),
                  (t == last[0]) & (j == last[1]))


def _item_specs(n_tiles):
    def tile(j, t, nu, kind):
        return jnp.where((kind[t] == 0) & (nu[t] > 0), j, n_tiles - 1)
    return lambda off: (lambda t, j, ie, row0, nu, kind: (ie[t], 0, off + tile(j, t, nu, kind)))


def _moe_scratch(k_dim, wb_cols, tn, stage_dtype):
    return ([pltpu.VMEM((MOE_ITEM_UNITS * MOE_TM, k_dim), BF16), pltpu.VMEM((k_dim, wb_cols), BF16)]
            + [pltpu.VMEM((units * MOE_TM, tn), stage_dtype) for units in SEG_UNITS]
            + [pltpu.SMEM((len(SEG_UNITS),), jnp.int32),
               pltpu.SemaphoreType.DMA(()),
               pltpu.SemaphoreType.DMA((len(SEG_UNITS),))])


def _gateup(xs, w_gate_up, b_gate_up, items):
    rows, d = xs.shape
    ne, _, two_f = w_gate_up.shape
    f = two_f // 2
    tn = MOE_UP_TN
    nt = f // tn
    wmap = _item_specs(nt)
    n_items = items[0].shape[0]
    return pl.pallas_call(
        functools.partial(_gateup_body, last=(n_items - 1, nt - 1)),
        grid_spec=pltpu.PrefetchScalarGridSpec(
            num_scalar_prefetch=4,
            grid=(n_items, nt),
            in_specs=[pl.BlockSpec(memory_space=pl.ANY),
                      pl.BlockSpec((None, d, tn), wmap(0)),
                      pl.BlockSpec((None, d, tn), wmap(nt)),
                      pl.BlockSpec((None, 1, tn), wmap(0)),
                      pl.BlockSpec((None, 1, tn), wmap(nt))],
            out_specs=pl.BlockSpec(memory_space=pl.ANY),
            scratch_shapes=_moe_scratch(d, 2 * tn, tn, BF16)),
        out_shape=jax.ShapeDtypeStruct((rows, f), BF16),
        compiler_params=_params("arbitrary", "arbitrary"),
        name="moe_gate_up",
    )(*items, xs, w_gate_up, w_gate_up,
      b_gate_up.reshape(ne, 1, two_f), b_gate_up.reshape(ne, 1, two_f))


def _down(act, w_down, b_down, items):
    rows = act.shape[0]
    ne, f, d = w_down.shape
    tn = MOE_DOWN_TN
    nt = d // tn
    wmap = _item_specs(nt)
    n_items = items[0].shape[0]
    return pl.pallas_call(
        functools.partial(_down_body, last=(n_items - 1, nt - 1)),
        grid_spec=pltpu.PrefetchScalarGridSpec(
            num_scalar_prefetch=4,
            grid=(n_items, nt),
            in_specs=[pl.BlockSpec(memory_space=pl.ANY),
                      pl.BlockSpec((None, f, tn), wmap(0)),
                      pl.BlockSpec((None, 1, tn), wmap(0))],
            out_specs=pl.BlockSpec(memory_space=pl.ANY),
            scratch_shapes=_moe_scratch(f, tn, tn, F32)),
        out_shape=jax.ShapeDtypeStruct((rows, d), F32),
        compiler_params=_params("arbitrary", "arbitrary"),
        name="moe_down",
    )(*items, act, w_down, b_down.reshape(ne, 1, d))


def _combine_body(dest_ref, dest_next_ref, y_hbm, x_ref, w_ref, g_ref, o_ref, buf_ref, sem, *, n_steps):
    i = pl.program_id(0)
    tb = x_ref.shape[0]

    def issue(idx_ref, slot):
        def body(t, carry):
            for k in range(TOP_K):
                _row_copy(y_hbm, buf_ref.at[slot, k], sem.at[slot], idx_ref[0, k, t], t).start()
            return carry
        lax.fori_loop(0, tb, body, 0, unroll=DMA_ISSUE_UNROLL // 2)

    @pl.when(i == 0)
    def _():
        issue(dest_ref, 0)

    @pl.when(i + 1 < n_steps)
    def _():
        issue(dest_next_ref, (i + 1) % 2)

    slot = i % 2
    for k in range(TOP_K):
        _rows_wait(y_hbm, buf_ref.at[slot, k], sem.at[slot])

    def group(c, carry):
        sl = pl.ds(pl.multiple_of(c * ROW_GROUP, ROW_GROUP), ROW_GROUP)
        w = w_ref[sl, :]
        y = buf_ref[slot, 0, sl, :] * w[:, 0:1]
        for k in range(1, TOP_K):
            y = y + buf_ref[slot, k, sl, :] * w[:, k:k + 1]
        o_ref[sl, :] = _rms(x_ref[sl, :] + y, g_ref[...])
        return carry
    lax.fori_loop(0, tb // ROW_GROUP, group, 0, unroll=ROW_GROUP_UNROLL)


def _combine(y_rows, x1, dest, top_w, g, tok0, n_tok):
    d = x1.shape[1]
    tb = COMBINE_TB
    blk0, nb = tok0 // tb, n_tok // tb
    dest_b = dest.reshape(TOP_K, -1, tb).transpose(1, 0, 2)
    dest_spec = lambda ahead: pl.BlockSpec(
        (1, TOP_K, tb), lambda i: (blk0 + jnp.minimum(i + ahead, nb - 1), 0, 0), memory_space=pltpu.SMEM)
    return pl.pallas_call(
        functools.partial(_combine_body, n_steps=nb),
        grid=(nb,),
        in_specs=[dest_spec(0), dest_spec(1),
                  pl.BlockSpec(memory_space=pl.ANY),
                  pl.BlockSpec((tb, d), lambda i: (blk0 + i, 0)),
                  pl.BlockSpec((tb, TOP_K), lambda i: (blk0 + i, 0)),
                  pl.BlockSpec((1, d), lambda i: (0, 0))],
        out_specs=pl.BlockSpec((tb, d), lambda i: (i, 0)),
        out_shape=jax.ShapeDtypeStruct((n_tok, d), F32),
        scratch_shapes=[pltpu.VMEM((2, TOP_K, tb, d), F32), pltpu.SemaphoreType.DMA((2,))],
        compiler_params=_params("arbitrary"),
        name="moe_combine_prompt" if tok0 == 0 else "moe_combine_sample",
    )(dest_b, dest_b, y_rows, x1, top_w.T, g.reshape(1, d))


def _pick(table, idx):
    hot = idx[..., None] == jnp.arange(table.shape[0], dtype=idx.dtype)
    return jnp.sum(jnp.where(hot, table, 0), axis=-1)


def _plan(top_e, rank, counts, n_units):
    u = MOE_TM
    n_tok = top_e.shape[1]
    i32 = jnp.int32
    units = (counts + u - 1) // u
    uend = jnp.cumsum(units)
    ustart = uend - units
    nvu = uend[-1]
    dest = _pick(ustart * u, top_e) + rank
    tok = jnp.broadcast_to(jnp.arange(n_tok, dtype=i32)[None, :], dest.shape)
    row_tok = jnp.zeros((n_units * u,), i32).at[dest.reshape(-1)].set(tok.reshape(-1))
    nvb = (nvu * u + DISPATCH_TM - 1) // DISPATCH_TM

    ne = counts.shape[0]
    tail = jnp.full((1,), 1, i32)
    busy_e = lax.cummax(jnp.where(units > 0, jnp.arange(ne, dtype=i32), 0))
    item_e = jnp.concatenate([busy_e, busy_e[-1:]])
    row0 = jnp.concatenate([ustart * u, tail * (nvu * u)])
    nu = jnp.concatenate([units, tail * (n_units - nvu)])
    kind = jnp.concatenate([jnp.zeros((ne,), i32), tail])
    items = tuple(a.astype(i32) for a in (item_e, row0, nu, kind))
    return dest, row_tok, nvb.astype(i32).reshape(1), items


def kernel(x_prompt, x_sample, cache_attn_k, cache_attn_v, state_ret, norm_mix, w_in, attn_sinks,
           w_o_attn, w_o_ret, w_out, norm_ffn, w_router, b_router, w_gate_up, b_gate_up,
           w_down, b_down, norm_final):
    bp, sp, d = x_prompt.shape
    bs, ss, _ = x_sample.shape
    depth = w_in.shape[0]
    assert depth == 1 and ss == CHUNK and sp % CHUNK == 0
    tp, ts = bp * sp, bs * ss
    t = tp + ts
    ncp = sp // CHUNK

    widths = (ATTN_Q_W, ATTN_KV_W, ATTN_KV_W, RET_W, RET_W, RET_W, RET_W, d, d)
    cols = [0]
    for wdt in widths:
        cols.append(cols[-1] + wdt)
    c_aq, c_ak, c_av, c_rq, c_rk, c_rv, c_rg, c_ga, c_gr = cols[:9]

    x = (x_prompt.reshape(tp, d), x_sample.reshape(ts, d))
    h = _rmsnorm(x[0], x[1], norm_mix[0], BF16)
    z = _matmul(h, w_in[0], F32, "in_proj")

    bias = _attn_bias(attn_sinks[0])
    n_win = cache_attn_k.shape[2]
    ck = cache_attn_k[0].reshape(bs * n_win, ATTN_KV_W)
    cv = cache_attn_v[0].reshape(bs * n_win, ATTN_KV_W)
    attn_p = _attention(z, bias, 0, bp, ncp, c_ak, c_av)
    attn_s = _attention(z, bias, tp // CHUNK, bs, 1, c_ak, c_av, ck, cv)

    consts = _ret_consts()
    r_p, state_p = _retention(z, consts, 0, bp, ncp, c_rq, c_rk, c_rv, c_rg)
    r_s, state_s = _retention(z, consts, tp // CHUNK, bs, 1, c_rq, c_rk, c_rv, c_rg, state_ret[0])

    merged = _merge((attn_p, attn_s), (r_p, r_s), w_o_attn[0], w_o_ret[0], z, c_ga, c_gr)
    x1 = _matmul(merged, w_out[0], F32, "out_proj", residual=x)

    hn, top_e, top_w, rank, cnt = _router(x1, norm_ffn[0], w_router[0], b_router[0])
    n_units = -(-(t * TOP_K) // MOE_TM) + N_EXPERTS
    n_units += -n_units % (DISPATCH_TM // MOE_TM)
    dest, row_tok, nvb, items = _plan(top_e, rank, cnt[:, 0], n_units)
    xs = _dispatch(hn, row_tok, nvb)
    act = _gateup(xs, w_gate_up[0], b_gate_up[0], items)
    y_rows = _down(act, w_down[0], b_down[0], items)
    y_prompt = _combine(y_rows, x1, dest, top_w, norm_final, 0, tp).reshape(bp, sp, d)
    y_sample = _combine(y_rows, x1, dest, top_w, norm_final, tp, ts).reshape(bs, ss, d)

    n_keep = min(WINDOW, sp)
    kv = z[:, c_ak:c_ak + 2 * ATTN_KV_W]
    kv_p = kv[:tp].reshape(bp, sp, 2, ATTN_KV_HEADS, ATTN_HEAD_DIM)[:, sp - n_keep:]
    kv_s = kv[tp:].reshape(bs, ss, 2, ATTN_KV_HEADS, ATTN_HEAD_DIM)
    return (y_prompt, y_sample, kv_p[:, :, 0][None], kv_p[:, :, 1][None], state_p[None],
            kv_s[:, :, 0][None], kv_s[:, :, 1][None], state_s[None])
```

```python
import functools

import jax
import jax.numpy as jnp
from jax import lax
from jax.experimental import pallas as pl
from jax.experimental.pallas import tpu as pltpu

F32 = jnp.float32
BF16 = jnp.bfloat16

CHUNK = 64
WINDOW = 128
ATTN_HEADS = 32
ATTN_KV_HEADS = 4
ATTN_GROUP = ATTN_HEADS // ATTN_KV_HEADS
ATTN_HEAD_DIM = 64
ATTN_Q_W = ATTN_HEADS * ATTN_HEAD_DIM
ATTN_KV_W = ATTN_KV_HEADS * ATTN_HEAD_DIM
RET_HEADS = 8
RET_HEAD_DIM = 256
RET_W = RET_HEADS * RET_HEAD_DIM
N_EXPERTS = 32
TOP_K = 4
SWIGLU_ALPHA = 1.702
SWIGLU_LIMIT = 7.0
RMS_EPS = 1e-5
NEG_INF = -1e30

ATTN_COLS = 256
RET_HEADS_PER_STEP = 2
RET_SEQS_PER_STEP = 4

VMEM_LIMIT_BYTES = 56 * 1024 * 1024

MOE_TM = 128
MOE_ITEM_UNITS = 16
DISPATCH_TM = 512
MOE_UP_TN = 256
MOE_DOWN_TN = 512
PROJ_TM = 1024
PROJ_TN = 512
MERGE_TM = 512
COMBINE_TB = 128
ROUTER_TB = 256
NORM_TM = 256


def _params(*semantics):
    return pltpu.CompilerParams(dimension_semantics=semantics, vmem_limit_bytes=VMEM_LIMIT_BYTES)


def _rms(x, g):
    ms = jnp.mean(x * x, axis=-1, keepdims=True)
    return x * lax.rsqrt(ms + RMS_EPS) * g


def _sigmoid(x):
    return 1.0 / (1.0 + jnp.exp(-x))


def _first(i, nf):
    return jnp.minimum(i, nf - 1)


def _second(i, nf):
    return jnp.maximum(i - nf, 0)


def _on_rows(i, nf, fn, first_refs, second_refs):
    @pl.when(i < nf)
    def _():
        fn(*first_refs)

    @pl.when(i >= nf)
    def _():
        fn(*second_refs)


def _rmsnorm_body(xp_ref, xs_ref, g_ref, o_ref, *, nf):
    def norm(x_ref):
        o_ref[...] = _rms(x_ref[...], g_ref[...]).astype(o_ref.dtype)

    _on_rows(pl.program_id(0), nf, norm, (xp_ref,), (xs_ref,))


def _rmsnorm(xp, xs, g, out_dtype):
    (mp, d), ms = xp.shape, xs.shape[0]
    tm = min(NORM_TM, ms)
    nf = mp // tm
    return pl.pallas_call(
        functools.partial(_rmsnorm_body, nf=nf),
        grid=((mp + ms) // tm,),
        in_specs=[pl.BlockSpec((tm, d), lambda i: (_first(i, nf), 0)),
                  pl.BlockSpec((tm, d), lambda i: (_second(i, nf), 0)),
                  pl.BlockSpec((1, d), lambda i: (0, 0))],
        out_specs=pl.BlockSpec((tm, d), lambda i: (i, 0)),
        out_shape=jax.ShapeDtypeStruct((mp + ms, d), out_dtype),
        compiler_params=_params("arbitrary"),
        name="rmsnorm",
    )(xp, xs, g.reshape(1, d))


def _matmul_body(x_ref, w_ref, *rest, nf):
    if nf is None:
        o_ref, wb_ref = rest
    else:
        rp_ref, rs_ref, o_ref, wb_ref = rest
    i = pl.program_id(1)

    @pl.when(i == 0)
    def _():
        wb_ref[...] = w_ref[...].astype(BF16)

    acc = jnp.dot(x_ref[...], wb_ref[...], preferred_element_type=F32)
    if nf is None:
        o_ref[...] = acc.astype(o_ref.dtype)
    else:
        def add(r_ref):
            o_ref[...] = (acc + r_ref[...]).astype(o_ref.dtype)

        _on_rows(i, nf, add, (rp_ref,), (rs_ref,))


def _matmul(x, w, out_dtype, name, residual=None):
    m, k = x.shape
    n = w.shape[1]
    tn = min(PROJ_TN, n)
    in_specs = [None, pl.BlockSpec((k, tn), lambda j, i: (0, j))]
    args = [x, w]
    nf = None
    if residual is None:
        tm = min(PROJ_TM, m)
    else:
        tm = min(PROJ_TM, residual[1].shape[0])
        nf = residual[0].shape[0] // tm
        in_specs += [pl.BlockSpec((tm, tn), lambda j, i: (_first(i, nf), j)),
                     pl.BlockSpec((tm, tn), lambda j, i: (_second(i, nf), j))]
        args += list(residual)
    in_specs[0] = pl.BlockSpec((tm, k), lambda j, i: (i, 0))
    return pl.pallas_call(
        functools.partial(_matmul_body, nf=nf),
        grid=(n // tn, m // tm),
        in_specs=in_specs,
        out_specs=pl.BlockSpec((tm, tn), lambda j, i: (i, j)),
        out_shape=jax.ShapeDtypeStruct((m, n), out_dtype),
        scratch_shapes=[pltpu.VMEM((k, tn), BF16)],
        compiler_params=_params("arbitrary", "arbitrary"),
        name=name,
    )(*args)


def _merge_body(ap_ref, as_ref, rp_ref, rs_ref, wa_ref, wr_ref, ga_ref, gr_ref, o_ref, wab_ref, wrb_ref, *, nf):
    i = pl.program_id(1)

    @pl.when(i == 0)
    def _():
        wab_ref[...] = wa_ref[...].astype(BF16)
        wrb_ref[...] = wr_ref[...].astype(BF16)

    def merge(a_ref, r_ref):
        ya = jnp.dot(a_ref[...], wab_ref[...], preferred_element_type=F32)
        yr = jnp.dot(r_ref[...], wrb_ref[...], preferred_element_type=F32)
        o_ref[...] = (_sigmoid(ga_ref[...]) * ya + _sigmoid(gr_ref[...]) * yr).astype(o_ref.dtype)

    _on_rows(i, nf, merge, (ap_ref, rp_ref), (as_ref, rs_ref))


def _merge(a, r, w_a, w_r, z, ga_col, gr_col):
    (mp, ka), ms = a[0].shape, a[1].shape[0]
    kr = r[0].shape[1]
    n = w_a.shape[1]
    tm, tn = min(MERGE_TM, ms), min(PROJ_TN, n)
    nf = mp // tm
    ga_blk, gr_blk = ga_col // tn, gr_col // tn
    rows = lambda k, pick: pl.BlockSpec((tm, k), lambda j, i: (pick(i, nf), 0))
    return pl.pallas_call(
        functools.partial(_merge_body, nf=nf),
        grid=(n // tn, (mp + ms) // tm),
        in_specs=[rows(ka, _first), rows(ka, _second), rows(kr, _first), rows(kr, _second),
                  pl.BlockSpec((ka, tn), lambda j, i: (0, j)),
                  pl.BlockSpec((kr, tn), lambda j, i: (0, j)),
                  pl.BlockSpec((tm, tn), lambda j, i: (i, ga_blk + j)),
                  pl.BlockSpec((tm, tn), lambda j, i: (i, gr_blk + j))],
        out_specs=pl.BlockSpec((tm, tn), lambda j, i: (i, j)),
        out_shape=jax.ShapeDtypeStruct((mp + ms, n), BF16),
        scratch_shapes=[pltpu.VMEM((ka, tn), BF16), pltpu.VMEM((kr, tn), BF16)],
        compiler_params=_params("arbitrary", "arbitrary"),
        name="gated_merge",
    )(a[0], a[1], r[0], r[1], w_a, w_r, z, z)


def _attn_body(q_ref, k0_ref, k1_ref, k2_ref, v0_ref, v1_ref, v2_ref, bias_ref, o_ref, *, prompt):
    hd, grp = ATTN_HEAD_DIM, ATTN_GROUP
    rows = grp * CHUNK
    n_keys = WINDOW + CHUNK
    pad = jnp.zeros((ATTN_COLS - n_keys, hd), F32)
    if prompt:
        n_invalid = jnp.maximum(WINDOW - pl.program_id(1) * CHUNK, 0)
        col = lax.broadcasted_iota(jnp.int32, (rows, ATTN_COLS), 1)
    for h in range(ATTN_KV_HEADS):
        sl = slice(h * hd, (h + 1) * hd)
        kh = jnp.concatenate([k0_ref[:, sl], k1_ref[:, sl], k2_ref[:, sl], pad], axis=0).astype(BF16)
        vh = jnp.concatenate([v0_ref[:, sl], v1_ref[:, sl], v2_ref[:, sl], pad], axis=0).astype(BF16)
        qh = jnp.concatenate(
            [q_ref[:, (h * grp + g) * hd:(h * grp + g + 1) * hd] for g in range(grp)], axis=0).astype(BF16)
        s = lax.dot_general(qh, kh, (((1,), (1,)), ((), ())), preferred_element_type=F32)
        s = s * (hd ** -0.5) + bias_ref[h * rows:(h + 1) * rows, :]
        if prompt:
            s = jnp.where(col < n_invalid, NEG_INF, s)
        m = jnp.max(s, axis=-1, keepdims=True)
        p = jnp.exp(s - m)
        p = (p / jnp.sum(p, axis=-1, keepdims=True)).astype(BF16)
        oh = jnp.dot(p, vh, preferred_element_type=F32)
        for g in range(grp):
            off = (h * grp + g) * hd
            o_ref[:, off:off + hd] = oh[g * CHUNK:(g + 1) * CHUNK, :].astype(o_ref.dtype)


def _attn_bias(sinks):
    n_keys = WINDOW + CHUNK
    slopes = 2.0 ** (-8.0 * jnp.arange(1, ATTN_HEADS + 1, dtype=F32) / ATTN_HEADS)
    q_pos = WINDOW + jnp.arange(CHUNK)
    k_pos = jnp.arange(n_keys)
    dist = jnp.abs(q_pos[:, None] - k_pos[None, :]).astype(F32)
    alibi = -(slopes[:, None, None] * dist[None])
    sink = jnp.broadcast_to(sinks.astype(F32)[:, None, None], (ATTN_HEADS, CHUNK, 1))
    fill = jnp.full((ATTN_HEADS, CHUNK, ATTN_COLS - n_keys - 1), NEG_INF, F32)
    return jnp.concatenate([alibi, sink, fill], axis=-1).reshape(ATTN_HEADS * CHUNK, ATTN_COLS)


def _attention(z, bias, row_blk0, n_seq, n_chunks, k_col, v_col, cache_k=None, cache_v=None):
    kb, vb = k_col // ATTN_KV_W, v_col // ATTN_KV_W
    wc = WINDOW // CHUNK

    def q_map(b, c):
        return (row_blk0 + b * n_chunks + c, 0)

    if cache_k is None:
        def kv_map(j, colblk):
            return lambda b, c: (row_blk0 + b * n_chunks + jnp.maximum(c - wc + j, 0), colblk)
        kv_args = [z] * 6
        kv_specs = [pl.BlockSpec((CHUNK, ATTN_KV_W), kv_map(j, cb)) for cb in (kb, vb) for j in range(wc + 1)]
    else:
        def cache_map(j):
            return lambda b, c: (b * wc + j, 0)
        def new_map(colblk):
            return lambda b, c: (row_blk0 + b, colblk)
        kv_args = [cache_k, cache_k, z, cache_v, cache_v, z]
        kv_specs = []
        for cb in (kb, vb):
            kv_specs += [pl.BlockSpec((CHUNK, ATTN_KV_W), cache_map(j)) for j in range(wc)]
            kv_specs.append(pl.BlockSpec((CHUNK, ATTN_KV_W), new_map(cb)))
    return pl.pallas_call(
        functools.partial(_attn_body, prompt=cache_k is None),
        grid=(n_seq, n_chunks),
        in_specs=[pl.BlockSpec((CHUNK, ATTN_Q_W), q_map)] + kv_specs
                 + [pl.BlockSpec(bias.shape, lambda b, c: (0, 0))],
        out_specs=pl.BlockSpec((CHUNK, ATTN_Q_W), lambda b, c: (b * n_chunks + c, 0)),
        out_shape=jax.ShapeDtypeStruct((n_seq * n_chunks * CHUNK, ATTN_Q_W), BF16),
        compiler_params=_params("arbitrary", "arbitrary"),
        name="attn_prompt" if cache_k is None else "attn_sample",
    )(z, *kv_args, bias)


def _ret_body(*refs, has_init, n_chunks):
    sps = RET_SEQS_PER_STEP
    z_refs = refs[:4 * sps]
    decay_ref, dout_ref, kd_ref, g64_ref = refs[4 * sps:4 * sps + 4]
    rest = refs[4 * sps + 4:]
    if has_init:
        s0_ref, r_ref, sout_ref, state_ref = rest
    else:
        r_ref, sout_ref, state_ref = rest
    c = pl.program_id(2)
    hd = RET_HEAD_DIM

    @pl.when(c == 0)
    def _():
        if has_init:
            state_ref[...] = s0_ref[...]
        else:
            state_ref[...] = jnp.zeros(state_ref.shape, F32)

    for s in range(sps):
        q_ref, k_ref, v_ref, g_ref = z_refs[4 * s:4 * s + 4]
        for hh in range(RET_HEADS_PER_STEP):
            sl = slice(hh * hd, (hh + 1) * hd)
            q = q_ref[:, sl].astype(BF16)
            k = k_ref[:, sl] * (hd ** -0.5)
            v = v_ref[:, sl].astype(BF16)
            st = state_ref[s, hh]
            scores = lax.dot_general(q, k.astype(BF16), (((1,), (1,)), ((), ())),
                                     preferred_element_type=F32) * decay_ref[hh]
            o = jnp.dot(scores.astype(BF16), v, preferred_element_type=F32)
            o = o + jnp.dot(q, st.astype(BF16), preferred_element_type=F32) * dout_ref[hh]
            k_dec = (k * kd_ref[hh]).T.astype(BF16)
            state_ref[s, hh] = st * g64_ref[hh] + jnp.dot(k_dec, v, preferred_element_type=F32)
            rn = o * lax.rsqrt(jnp.mean(o * o, axis=-1, keepdims=True) + RMS_EPS)
            gate = g_ref[:, sl]
            r_ref[s, :, sl] = (rn * (gate * _sigmoid(gate))).astype(r_ref.dtype)

    @pl.when(c == n_chunks - 1)
    def _():
        sout_ref[...] = state_ref[...]


def _ret_consts():
    log_g = jnp.log1p(-(2.0 ** (-5.0 - jnp.arange(RET_HEADS, dtype=F32))))
    idx = jnp.arange(CHUNK, dtype=F32)
    rel = idx[:, None] - idx[None, :]
    decay = jnp.where(rel >= 0, jnp.exp(log_g[:, None, None] * jnp.maximum(rel, 0.0)), 0.0)
    dout = jnp.exp((idx[None, :] + 1.0) * log_g[:, None])
    kd = jnp.exp((CHUNK - 1.0 - idx)[None, :] * log_g[:, None])
    g64 = jnp.exp(CHUNK * log_g)
    bc = lambda a: jnp.broadcast_to(a[:, :, None], (RET_HEADS, a.shape[1], RET_HEAD_DIM))
    return decay, bc(dout), bc(kd), bc(g64[:, None])


def _retention(z, consts, row_blk0, n_seq, n_chunks, q_col, k_col, v_col, g_col, state0=None):
    hps, sps = RET_HEADS_PER_STEP, RET_SEQS_PER_STEP
    w = hps * RET_HEAD_DIM
    n_hp = RET_HEADS // hps
    decay, dout, kd, g64 = consts

    def zmap(s, col):
        cb = col // w
        return lambda b, h, c: (row_blk0 + (b * sps + s) * n_chunks + c, cb + h)

    cmap = lambda b, h, c: (h, 0, 0)
    in_specs = [pl.BlockSpec((CHUNK, w), zmap(s, col)) for s in range(sps) for col in (q_col, k_col, v_col, g_col)]
    in_specs += [pl.BlockSpec((hps, CHUNK, CHUNK), cmap),
                 pl.BlockSpec((hps, CHUNK, RET_HEAD_DIM), cmap),
                 pl.BlockSpec((hps, CHUNK, RET_HEAD_DIM), cmap),
                 pl.BlockSpec((hps, 1, RET_HEAD_DIM), cmap)]
    args = [z] * (4 * sps) + [decay, dout, kd, g64]
    st_spec = pl.BlockSpec((sps, hps, RET_HEAD_DIM, RET_HEAD_DIM), lambda b, h, c: (b, h, 0, 0))
    if state0 is not None:
        in_specs.append(st_spec)
        args.append(state0)
    r, state = pl.pallas_call(
        functools.partial(_ret_body, has_init=state0 is not None, n_chunks=n_chunks),
        grid=(n_seq // sps, n_hp, n_chunks),
        in_specs=in_specs,
        out_specs=[pl.BlockSpec((sps, CHUNK, w), lambda b, h, c: (b, c, h)), st_spec],
        out_shape=[jax.ShapeDtypeStruct((n_seq, n_chunks * CHUNK, RET_W), BF16),
                   jax.ShapeDtypeStruct((n_seq, RET_HEADS, RET_HEAD_DIM, RET_HEAD_DIM), F32)],
        scratch_shapes=[pltpu.VMEM((sps, hps, RET_HEAD_DIM, RET_HEAD_DIM), F32)],
        compiler_params=_params("arbitrary", "arbitrary", "arbitrary"),
        name="retention_prompt" if state0 is None else "retention_sample",
    )(*args)
    return r.reshape(n_seq * n_chunks * CHUNK, RET_W), state


def _router_body(x_ref, g_ref, whi_ref, wlo_ref, b_ref, tri_ref, hn_ref, e_ref, w_ref, rank_ref, cnt_ref,
                 carry_ref):
    ne, tb = b_ref.shape

    @pl.when(pl.program_id(0) == 0)
    def _():
        carry_ref[...] = jnp.zeros(carry_ref.shape, F32)

    hn = _rms(x_ref[...], g_ref[...])
    hn_ref[...] = hn
    hi = hn.astype(BF16)
    lo = (hn - hi.astype(F32)).astype(BF16)
    dn = (((1,), (1,)), ((), ()))
    logits = (lax.dot_general(whi_ref[...], hi, dn, preferred_element_type=F32)
              + lax.dot_general(whi_ref[...], lo, dn, preferred_element_type=F32)
              + lax.dot_general(wlo_ref[...], hi, dn, preferred_element_type=F32)) + b_ref[...]
    eidx = lax.broadcasted_iota(jnp.int32, (ne, tb), 0)
    rest = logits
    tops, idxs, hots = [], [], []
    for _ in range(TOP_K):
        m = jnp.max(rest, axis=0, keepdims=True)
        idx = jnp.min(jnp.where(rest == m, eidx, ne), axis=0, keepdims=True)
        hot = eidx == idx
        tops.append(m)
        idxs.append(idx)
        hots.append(hot)
        rest = jnp.where(hot, -jnp.inf, rest)
    ex = [jnp.exp(t - tops[0]) for t in tops]
    den = ex[0] + ex[1] + ex[2] + ex[3]
    member = jnp.zeros((ne, tb), F32)
    for hot in hots:
        member = member + jnp.where(hot, 1.0, 0.0)
    before = jnp.dot(member.astype(BF16), tri_ref[...], preferred_element_type=F32)
    carry = carry_ref[...]
    base = before + jnp.concatenate([carry] * (tb // carry.shape[1]), axis=1)
    ranks = [jnp.sum(jnp.where(hot, base, 0.0), axis=0, keepdims=True) for hot in hots]
    e_ref[...] = jnp.concatenate(idxs, axis=0)
    w_ref[...] = jnp.concatenate([e / den for e in ex], axis=0)
    rank_ref[...] = jnp.concatenate(ranks, axis=0).astype(jnp.int32)
    carry = carry + jnp.sum(member, axis=1, keepdims=True)
    carry_ref[...] = carry
    cnt_ref[...] = carry.astype(jnp.int32)


def _router(x1, g, w_router, b_router):
    t, d = x1.shape
    ne = w_router.shape[1]
    tb = ROUTER_TB
    wt = w_router.T
    wt_hi = wt.astype(BF16)
    wt_lo = (wt - wt_hi.astype(F32)).astype(BF16)
    bias = jnp.broadcast_to(b_router.astype(F32)[:, None], (ne, tb))
    tri = (jnp.arange(tb)[:, None] < jnp.arange(tb)[None, :]).astype(BF16)
    const = lambda shape: pl.BlockSpec(shape, lambda i: (0,) * len(shape))
    col = pl.BlockSpec((TOP_K, tb), lambda i: (0, i))
    return pl.pallas_call(
        _router_body,
        grid=(t // tb,),
        in_specs=[pl.BlockSpec((tb, d), lambda i: (i, 0)), const((1, d)), const((ne, d)), const((ne, d)),
                  const((ne, tb)), const((tb, tb))],
        out_specs=[pl.BlockSpec((tb, d), lambda i: (i, 0)), col, col, col, const((ne, 128))],
        out_shape=[jax.ShapeDtypeStruct((t, d), F32),
                   jax.ShapeDtypeStruct((TOP_K, t), jnp.int32),
                   jax.ShapeDtypeStruct((TOP_K, t), F32),
                   jax.ShapeDtypeStruct((TOP_K, t), jnp.int32),
                   jax.ShapeDtypeStruct((ne, 128), jnp.int32)],
        scratch_shapes=[pltpu.VMEM((ne, 128), F32)],
        compiler_params=_params("arbitrary"),
        name="router",
    )(x1, g.reshape(1, d), wt_hi, wt_lo, bias, tri)


DMA_ISSUE_UNROLL = 8
ROW_GROUP = 8
ROW_GROUP_UNROLL = 4


def _row_copy(src_hbm, dst, sem, src_row, dst_row):
    return pltpu.make_async_copy(src_hbm.at[pl.ds(src_row, 1), :], dst.at[pl.ds(dst_row, 1), :], sem)


def _rows_wait(src_hbm, dst, sem):
    pltpu.make_async_copy(src_hbm.at[pl.ds(0, dst.shape[0]), :], dst, sem).wait()


def _dispatch_body(nvb_ref, tok_ref, tok_next_ref, x_hbm, o_ref, buf_ref, sem):
    i = pl.program_id(0)
    nvb = nvb_ref[0]
    tm = buf_ref.shape[1]

    def issue(idx_ref, slot):
        def body(r, carry):
            _row_copy(x_hbm, buf_ref.at[slot], sem.at[slot], idx_ref[0, 0, r], r).start()
            return carry
        lax.fori_loop(0, tm, body, 0, unroll=DMA_ISSUE_UNROLL)

    @pl.when(i == 0)
    def _():
        issue(tok_ref, 0)

    @pl.when(i + 1 < nvb)
    def _():
        issue(tok_next_ref, (i + 1) % 2)

    @pl.when(i < nvb)
    def _():
        slot = i % 2
        _rows_wait(x_hbm, buf_ref.at[slot], sem.at[slot])

        o_ref[...] = buf_ref[slot].astype(o_ref.dtype)

    @pl.when(i >= nvb)
    def _():
        o_ref[...] = jnp.zeros(o_ref.shape, o_ref.dtype)


def _dispatch(hn, row_tok, nvb):
    t, d = hn.shape
    tm = DISPATCH_TM
    n_blocks = row_tok.shape[0] // tm
    tok_spec = lambda ahead: pl.BlockSpec(
        (1, 1, tm), lambda i, nvb_ref: (jnp.minimum(i + ahead, nvb_ref[0] - 1), 0, 0), memory_space=pltpu.SMEM)
    row_tok = row_tok.reshape(n_blocks, 1, tm)
    return pl.pallas_call(
        _dispatch_body,
        grid_spec=pltpu.PrefetchScalarGridSpec(
            num_scalar_prefetch=1,
            grid=(n_blocks,),
            in_specs=[tok_spec(0), tok_spec(1), pl.BlockSpec(memory_space=pl.ANY)],
            out_specs=pl.BlockSpec((tm, d), lambda i, nvb_ref: (i, 0)),
            scratch_shapes=[pltpu.VMEM((2, tm, d), F32), pltpu.SemaphoreType.DMA((2,))]),
        out_shape=jax.ShapeDtypeStruct((n_blocks * tm, d), BF16),
        compiler_params=_params("arbitrary"),
        name="moe_dispatch",
    )(nvb, row_tok, row_tok, hn)


SEG_UNITS = (8, 8, 4, 2, 1)


def _segments(nu):
    n8 = nu >> 3
    b4, b2, b1 = nu & 4, nu & 2, nu & 1
    off4 = n8 * 8
    active = (n8 >= 1, n8 >= 2, b4 != 0, b2 != 0, b1 != 0)
    first = (0, 8, off4, off4 + b4, off4 + b4 + b2)
    return active, first


def _unit_rows(base, first_unit, n_units):
    if isinstance(first_unit, int) and isinstance(base, int):
        return pl.ds(base + first_unit * MOE_TM, n_units * MOE_TM)
    return pl.ds(pl.multiple_of(base + first_unit * MOE_TM, MOE_TM), n_units * MOE_TM)


def _for_units(nu, fn):
    def body(u, carry):
        fn(u)
        return carry
    lax.fori_loop(0, nu, body, 0)


def _process_item(nu_total, compute, reload, load_rows, prepare, produce, out_copy, stage_refs, flag_ref,
                  is_first_step, is_last_step):
    cap = MOE_ITEM_UNITS
    n_pass = (nu_total + cap - 1) >> (cap.bit_length() - 1)

    @pl.when(is_first_step)
    def _():
        for k in range(len(SEG_UNITS)):
            flag_ref[k] = 0

    def one_pass(p, carry):
        nu = jnp.minimum(nu_total - p * cap, cap)
        first_row = p * (cap * MOE_TM)

        @pl.when(compute & (reload | (n_pass > 1)))
        def _():
            load_rows(first_row, nu)

        _stream_segments(nu, compute, prepare, produce,
                         lambda k, first_unit, units, to_item: out_copy(k, first_row, first_unit, units, to_item),
                         stage_refs, flag_ref)
        return carry
    lax.fori_loop(0, n_pass, one_pass, 0)

    @pl.when(is_last_step)
    def _():
        for k, units in enumerate(SEG_UNITS):
            @pl.when(flag_ref[k] == 1)
            def _(k=k, units=units):
                out_copy(k, 0, 0, units, None).wait()
                flag_ref[k] = 0


def _stream_segments(nu, compute, prepare, produce, out_copy, stage_refs, flag_ref):
    active, first = _segments(nu)

    @pl.when(compute & (nu > 0) & jnp.logical_not(active[0]))
    def _():
        prepare()

    for k, (units, stage_ref) in enumerate(zip(SEG_UNITS, stage_refs)):
        @pl.when(active[k])
        def _(k=k, units=units, stage_ref=stage_ref):
            @pl.when(flag_ref[k] == 1)
            def _():
                out_copy(k, 0, units, None).wait()

            @pl.when(compute)
            def _():
                if k == 0:
                    prepare()
                stage_ref[...] = produce(first[k], units).astype(stage_ref.dtype)

            @pl.when(jnp.logical_not(compute))
            def _():
                stage_ref[...] = jnp.zeros(stage_ref.shape, stage_ref.dtype)

            out_copy(k, first[k], units, True).start()
            flag_ref[k] = 1


def _gateup_body(ie_ref, row0_ref, nu_ref, kind_ref, x_hbm, wg_ref, wl_ref, bg_ref, bl_ref, act_hbm,
                 xbuf_ref, wb_ref, st0, st1, st2, st3, st4, flag_ref, sem_x, sem_o, *, last):
    t, j = pl.program_id(0), pl.program_id(1)
    nu_total, r0 = nu_ref[t], row0_ref[t]
    compute = kind_ref[t] == 0
    tn = wg_ref.shape[1]
    stages = (st0, st1, st2, st3, st4)

    def load_rows(first_row, nu):
        def x_copy(u):
            return pltpu.make_async_copy(x_hbm.at[_unit_rows(r0 + first_row, u, 1), :],
                                         xbuf_ref.at[_unit_rows(0, u, 1), :], sem_x)
        _for_units(nu, lambda u: x_copy(u).start())
        _for_units(nu, lambda u: x_copy(u).wait())

    def out_copy(k, first_row, first_unit, units, to_item):
        rows = _unit_rows(r0 + first_row, first_unit, units) if to_item else pl.ds(0, units * MOE_TM)
        cols = pl.ds(pl.multiple_of(j * tn, tn), tn)
        return pltpu.make_async_copy(stages[k], act_hbm.at[rows, cols], sem_o.at[k])

    def prepare():
        wb_ref[:, :tn] = wg_ref[...].astype(BF16)
        wb_ref[:, tn:] = wl_ref[...].astype(BF16)

    def produce(first_unit, units):
        gu = jnp.dot(xbuf_ref[_unit_rows(0, first_unit, units), :], wb_ref[...], preferred_element_type=F32)
        glu = jnp.minimum(gu[:, :tn] + bg_ref[...], SWIGLU_LIMIT)
        lin = jnp.clip(gu[:, tn:] + bl_ref[...], -SWIGLU_LIMIT, SWIGLU_LIMIT)
        return glu * _sigmoid(SWIGLU_ALPHA * glu) * (lin + 1.0)

    _process_item(nu_total, compute, j == 0, load_rows, prepare, produce, out_copy, stages, flag_ref,
                  (t == 0) & (j == 0),
                  (t == last[0]) & (j == last[1]))


def _down_body(ie_ref, row0_ref, nu_ref, kind_ref, a_hbm, w_ref, b_ref, y_hbm,
               abuf_ref, wb_ref, st0, st1, st2, st3, st4, flag_ref, sem_a, sem_o, *, last):
    t, j = pl.program_id(0), pl.program_id(1)
    nu_total, r0 = nu_ref[t], row0_ref[t]
    compute = kind_ref[t] == 0
    tn = w_ref.shape[1]
    stages = (st0, st1, st2, st3, st4)

    def load_rows(first_row, nu):
        def a_copy(u):
            return pltpu.make_async_copy(a_hbm.at[_unit_rows(r0 + first_row, u, 1), :],
                                         abuf_ref.at[_unit_rows(0, u, 1), :], sem_a)
        _for_units(nu, lambda u: a_copy(u).start())
        _for_units(nu, lambda u: a_copy(u).wait())

    def out_copy(k, first_row, first_unit, units, to_item):
        rows = _unit_rows(r0 + first_row, first_unit, units) if to_item else pl.ds(0, units * MOE_TM)
        cols = pl.ds(pl.multiple_of(j * tn, tn), tn)
        return pltpu.make_async_copy(stages[k], y_hbm.at[rows, cols], sem_o.at[k])

    def prepare():
        wb_ref[...] = w_ref[...].astype(BF16)

    def produce(first_unit, units):
        return jnp.dot(abuf_ref[_unit_rows(0, first_unit, units), :], wb_ref[...],
                       preferred_element_type=F32) + b_ref[...]

    _process_item(nu_total, compute, j == 0, load_rows, prepare, produce, out_copy, stages, flag_ref,
                  (t == 0) & (j == 0),
                  (t == last[0]) & (j == last[1]))


def _item_specs(n_tiles):
    def tile(j, t, nu, kind):
        return jnp.where((kind[t] == 0) & (nu[t] > 0), j, n_tiles - 1)
    return lambda off: (lambda t, j, ie, row0, nu, kind: (ie[t], 0, off + tile(j, t, nu, kind)))


def _moe_scratch(k_dim, wb_cols, tn, stage_dtype):
    return ([pltpu.VMEM((MOE_ITEM_UNITS * MOE_TM, k_dim), BF16), pltpu.VMEM((k_dim, wb_cols), BF16)]
            + [pltpu.VMEM((units * MOE_TM, tn), stage_dtype) for units in SEG_UNITS]
            + [pltpu.SMEM((len(SEG_UNITS),), jnp.int32),
               pltpu.SemaphoreType.DMA(()),
               pltpu.SemaphoreType.DMA((len(SEG_UNITS),))])


def _gateup(xs, w_gate_up, b_gate_up, items):
    rows, d = xs.shape
    ne, _, two_f = w_gate_up.shape
    f = two_f // 2
    tn = MOE_UP_TN
    nt = f // tn
    wmap = _item_specs(nt)
    n_items = items[0].shape[0]
    return pl.pallas_call(
        functools.partial(_gateup_body, last=(n_items - 1, nt - 1)),
        grid_spec=pltpu.PrefetchScalarGridSpec(
            num_scalar_prefetch=4,
            grid=(n_items, nt),
            in_specs=[pl.BlockSpec(memory_space=pl.ANY),
                      pl.BlockSpec((None, d, tn), wmap(0)),
                      pl.BlockSpec((None, d, tn), wmap(nt)),
                      pl.BlockSpec((None, 1, tn), wmap(0)),
                      pl.BlockSpec((None, 1, tn), wmap(nt))],
            out_specs=pl.BlockSpec(memory_space=pl.ANY),
            scratch_shapes=_moe_scratch(d, 2 * tn, tn, BF16)),
        out_shape=jax.ShapeDtypeStruct((rows, f), BF16),
        compiler_params=_params("arbitrary", "arbitrary"),
        name="moe_gate_up",
    )(*items, xs, w_gate_up, w_gate_up,
      b_gate_up.reshape(ne, 1, two_f), b_gate_up.reshape(ne, 1, two_f))


def _down(act, w_down, b_down, items):
    rows = act.shape[0]
    ne, f, d = w_down.shape
    tn = MOE_DOWN_TN
    nt = d // tn
    wmap = _item_specs(nt)
    n_items = items[0].shape[0]
    return pl.pallas_call(
        functools.partial(_down_body, last=(n_items - 1, nt - 1)),
        grid_spec=pltpu.PrefetchScalarGridSpec(
            num_scalar_prefetch=4,
            grid=(n_items, nt),
            in_specs=[pl.BlockSpec(memory_space=pl.ANY),
                      pl.BlockSpec((None, f, tn), wmap(0)),
                      pl.BlockSpec((None, 1, tn), wmap(0))],
            out_specs=pl.BlockSpec(memory_space=pl.ANY),
            scratch_shapes=_moe_scratch(f, tn, tn, F32)),
        out_shape=jax.ShapeDtypeStruct((rows, d), F32),
        compiler_params=_params("arbitrary", "arbitrary"),
        name="moe_down",
    )(*items, act, w_down, b_down.reshape(ne, 1, d))


def _combine_body(dest_ref, dest_next_ref, y_hbm, x_ref, w_ref, g_ref, o_ref, buf_ref, sem, *, n_steps):
    i = pl.program_id(0)
    tb = x_ref.shape[0]

    def issue(idx_ref, slot):
        def body(t, carry):
            for k in range(TOP_K):
                _row_copy(y_hbm, buf_ref.at[slot, k], sem.at[slot], idx_ref[0, k, t], t).start()
            return carry
        lax.fori_loop(0, tb, body, 0, unroll=DMA_ISSUE_UNROLL // 2)

    @pl.when(i == 0)
    def _():
        issue(dest_ref, 0)

    @pl.when(i + 1 < n_steps)
    def _():
        issue(dest_next_ref, (i + 1) % 2)

    slot = i % 2
    for k in range(TOP_K):
        _rows_wait(y_hbm, buf_ref.at[slot, k], sem.at[slot])

    def group(c, carry):
        sl = pl.ds(pl.multiple_of(c * ROW_GROUP, ROW_GROUP), ROW_GROUP)
        w = w_ref[sl, :]
        y = buf_ref[slot, 0, sl, :] * w[:, 0:1]
        for k in range(1, TOP_K):
            y = y + buf_ref[slot, k, sl, :] * w[:, k:k + 1]
        o_ref[sl, :] = _rms(x_ref[sl, :] + y, g_ref[...])
        return carry
    lax.fori_loop(0, tb // ROW_GROUP, group, 0, unroll=ROW_GROUP_UNROLL)


def _combine(y_rows, x1, dest, top_w, g, tok0, n_tok):
    d = x1.shape[1]
    tb = COMBINE_TB
    blk0, nb = tok0 // tb, n_tok // tb
    dest_b = dest.reshape(TOP_K, -1, tb).transpose(1, 0, 2)
    dest_spec = lambda ahead: pl.BlockSpec(
        (1, TOP_K, tb), lambda i: (blk0 + jnp.minimum(i + ahead, nb - 1), 0, 0), memory_space=pltpu.SMEM)
    return pl.pallas_call(
        functools.partial(_combine_body, n_steps=nb),
        grid=(nb,),
        in_specs=[dest_spec(0), dest_spec(1),
                  pl.BlockSpec(memory_space=pl.ANY),
                  pl.BlockSpec((tb, d), lambda i: (blk0 + i, 0)),
                  pl.BlockSpec((tb, TOP_K), lambda i: (blk0 + i, 0)),
                  pl.BlockSpec((1, d), lambda i: (0, 0))],
        out_specs=pl.BlockSpec((tb, d), lambda i: (i, 0)),
        out_shape=jax.ShapeDtypeStruct((n_tok, d), F32),
        scratch_shapes=[pltpu.VMEM((2, TOP_K, tb, d), F32), pltpu.SemaphoreType.DMA((2,))],
        compiler_params=_params("arbitrary"),
        name="moe_combine_prompt" if tok0 == 0 else "moe_combine_sample",
    )(dest_b, dest_b, y_rows, x1, top_w.T, g.reshape(1, d))


def _row_token_body(dest_ref, lo_ref, hi_ref, o_ref, *, n_tok):
    def clear(r, carry):
        o_ref[r] = 0
        return carry
    for e in range(lo_ref.shape[0]):
        lax.fori_loop(lo_ref[e], hi_ref[e], clear, 0)

    def put(t, carry):
        for k in range(TOP_K):
            o_ref[dest_ref[k * n_tok + t]] = t
        return carry
    lax.fori_loop(0, n_tok, put, 0, unroll=4)


def _row_token(dest, pad_lo, pad_hi, n_rows):
    n_tok = dest.shape[1]
    smem = pl.BlockSpec(memory_space=pltpu.SMEM)
    return pl.pallas_call(
        functools.partial(_row_token_body, n_tok=n_tok),
        in_specs=[smem, smem, smem],
        out_specs=smem,
        out_shape=jax.ShapeDtypeStruct((n_rows,), jnp.int32),
        name="moe_row_token",
    )(dest.reshape(-1), pad_lo, pad_hi)


def _pick(table, idx):
    hot = idx[..., None] == jnp.arange(table.shape[0], dtype=idx.dtype)
    return jnp.sum(jnp.where(hot, table, 0), axis=-1)


def _plan(top_e, rank, counts, n_units):
    u = MOE_TM
    n_tok = top_e.shape[1]
    i32 = jnp.int32
    units = (counts + u - 1) // u
    uend = jnp.cumsum(units)
    ustart = uend - units
    nvu = uend[-1]
    dest = _pick(ustart * u, top_e) + rank
    n_rows = n_units * u
    pad_lo = jnp.concatenate([ustart * u + counts, (nvu * u).reshape(1)]).astype(i32)
    pad_hi = jnp.concatenate([uend * u, jnp.full((1,), n_rows, i32)]).astype(i32)
    row_tok = _row_token(dest, pad_lo, pad_hi, n_rows)
    nvb =(nvu * u + DISPATCH_TM - 1) // DISPATCH_TM

    ne = counts.shape[0]
    tail = jnp.full((1,), 1, i32)
    busy_e = lax.cummax(jnp.where(units > 0, jnp.arange(ne, dtype=i32), 0))
    item_e = jnp.concatenate([busy_e, busy_e[-1:]])
    row0 = jnp.concatenate([ustart * u, tail * (nvu * u)])
    nu = jnp.concatenate([units, tail * (n_units - nvu)])
    kind = jnp.concatenate([jnp.zeros((ne,), i32), tail])
    items = tuple(a.astype(i32) for a in (item_e, row0, nu, kind))
    return dest, row_tok, nvb.astype(i32).reshape(1), items


def kernel(x_prompt, x_sample, cache_attn_k, cache_attn_v, state_ret, norm_mix, w_in, attn_sinks,
           w_o_attn, w_o_ret, w_out, norm_ffn, w_router, b_router, w_gate_up, b_gate_up,
           w_down, b_down, norm_final):
    bp, sp, d = x_prompt.shape
    bs, ss, _ = x_sample.shape
    depth = w_in.shape[0]
    assert depth == 1 and ss == CHUNK and sp % CHUNK == 0
    tp, ts = bp * sp, bs * ss
    t = tp + ts
    ncp = sp // CHUNK

    widths = (ATTN_Q_W, ATTN_KV_W, ATTN_KV_W, RET_W, RET_W, RET_W, RET_W, d, d)
    cols = [0]
    for wdt in widths:
        cols.append(cols[-1] + wdt)
    c_aq, c_ak, c_av, c_rq, c_rk, c_rv, c_rg, c_ga, c_gr = cols[:9]

    x = (x_prompt.reshape(tp, d), x_sample.reshape(ts, d))
    h = _rmsnorm(x[0], x[1], norm_mix[0], BF16)
    z = _matmul(h, w_in[0], F32, "in_proj")

    bias = _attn_bias(attn_sinks[0])
    n_win = cache_attn_k.shape[2]
    ck = cache_attn_k[0].reshape(bs * n_win, ATTN_KV_W)
    cv = cache_attn_v[0].reshape(bs * n_win, ATTN_KV_W)
    attn_p = _attention(z, bias, 0, bp, ncp, c_ak, c_av)
    attn_s = _attention(z, bias, tp // CHUNK, bs, 1, c_ak, c_av, ck, cv)

    consts = _ret_consts()
    r_p, state_p = _retention(z, consts, 0, bp, ncp, c_rq, c_rk, c_rv, c_rg)
    r_s, state_s = _retention(z, consts, tp // CHUNK, bs, 1, c_rq, c_rk, c_rv, c_rg, state_ret[0])

    merged = _merge((attn_p, attn_s), (r_p, r_s), w_o_attn[0], w_o_ret[0], z, c_ga, c_gr)
    x1 = _matmul(merged, w_out[0], F32, "out_proj", residual=x)

    hn, top_e, top_w, rank, cnt = _router(x1, norm_ffn[0], w_router[0], b_router[0])
    n_units = -(-(t * TOP_K) // MOE_TM) + N_EXPERTS
    n_units += -n_units % (DISPATCH_TM // MOE_TM)
    dest, row_tok, nvb, items = _plan(top_e, rank, cnt[:, 0], n_units)
    xs = _dispatch(hn, row_tok, nvb)
    act = _gateup(xs, w_gate_up[0], b_gate_up[0], items)
    y_rows = _down(act, w_down[0], b_down[0], items)
    y_prompt = _combine(y_rows, x1, dest, top_w, norm_final, 0, tp).reshape(bp, sp, d)
    y_sample = _combine(y_rows, x1, dest, top_w, norm_final, tp, ts).reshape(bs, ss, d)

    n_keep = min(WINDOW, sp)
    kv = z[:, c_ak:c_ak + 2 * ATTN_KV_W]
    kv_p = kv[:tp].reshape(bp, sp, 2, ATTN_KV_HEADS, ATTN_HEAD_DIM)[:, sp - n_keep:]
    kv_s = kv[tp:].reshape(bs, ss, 2, ATTN_KV_HEADS, ATTN_HEAD_DIM)
    return (y_prompt, y_sample, kv_p[:, :, 0][None], kv_p[:, :, 1][None], state_p[None],
            kv_s[:, :, 0][None], kv_s[:, :, 1][None], state_s[None])
```
